```python
import math
import jax, jax.numpy as jnp
from jax import lax
import numpy as np

D_MODEL = 4096
BATCH = 4
SEQ = 4096
DEPTH = 2

GRID_W = 64
CTX_LEN = 256

C_CONV = 1024
CONV_W = 31
N_HEADS_R = 16
HEAD_R = 64
C_R = N_HEADS_R * HEAD_R
W_LORA = 64
A_LORA = 64
G_LORA = 160
GN_EPS = 64e-5
N_HEADS_M = 16
NOPE = 128
ROPE = 64
V_HEAD = 128
Q_LORA = 1536
KV_LORA = 512
C_M = N_HEADS_M * V_HEAD
MIX = C_CONV + C_R + C_M
ROPE_BASE = 10000.0
ATTN_SCALE = (NOPE + ROPE) ** -0.5
Q_BLOCK = 128

OFF_CONV = 0
OFF_MQ = OFF_CONV + 2 * C_CONV
OFF_RR = OFF_MQ + Q_LORA
OFF_RG = OFF_RR + C_R
OFF_RK = OFF_RG + G_LORA
OFF_RV = OFF_RK + C_R
OFF_RW = OFF_RV + C_R
OFF_RA = OFF_RW + 2 * W_LORA
OFF_MKV = OFF_RA + 2 * A_LORA
OFF_MKR = OFF_MKV + KV_LORA
IN_COLS = OFF_MKR + ROPE
RWKV_COLS = OFF_MKV - OFF_RR

N_EXPERTS = 16
N_GROUPS = 4
EXPERTS_PER_GROUP = N_EXPERTS // N_GROUPS
TOP_K = 2
D_EXPERT = 1024
MOE_BLOCK = 128

kernel_name = "hybrid_dit_conv_rwkv7_mla_moe"


def rmsnorm(x, g, eps=1e-6):
    xf = x.astype(jnp.float32)
    y = xf * lax.rsqrt(jnp.mean(xf * xf, -1, keepdims=True) + eps)
    return (y * g.astype(jnp.float32)).astype(x.dtype)


def layernorm(x, g, b, eps=1e-5):
    xf = x.astype(jnp.float32)
    mu = jnp.mean(xf, -1, keepdims=True)
    var = jnp.mean(jnp.square(xf - mu), -1, keepdims=True)
    return ((xf - mu) * lax.rsqrt(var + eps) * g.astype(jnp.float32) + b.astype(jnp.float32)).astype(x.dtype)


def col(p, base, lo, hi):
    return p[..., lo - base:hi - base]


def rope2d_tables(rows):
    half = ROPE // 2
    inv_freq = ROPE_BASE ** (-jnp.arange(0, half, 2, dtype=jnp.float32) / half)
    r = jnp.repeat(jnp.arange(rows, dtype=jnp.float32), GRID_W)
    cl = jnp.tile(jnp.arange(GRID_W, dtype=jnp.float32), rows)
    ar = r[:, None] * inv_freq
    ac = cl[:, None] * inv_freq
    ang = jnp.concatenate([ar, ar, ac, ac], -1)
    return jnp.cos(ang), jnp.sin(ang)


def apply_rope2d(x, cos, sin):
    x1, x2, x3, x4 = jnp.split(x, 4, axis=-1)
    rot = jnp.concatenate([-x2, x1, -x4, x3], -1)
    return (x * cos + rot * sin).astype(x.dtype)


def conformer_conv(u, dw, db, ln_g, ln_b):
    a, gate = jnp.split(u, 2, axis=-1)
    hdn = a * jax.nn.sigmoid(gate)
    hdn = lax.conv_general_dilated(hdn, dw[:, None, :], window_strides=(1,),
                                   padding=((CONV_W // 2, CONV_W // 2),),
                                   dimension_numbers=("NWC", "WIO", "NWC"),
                                   feature_group_count=C_CONV) + db
    return jax.nn.silu(layernorm(hdn, ln_g, ln_b))


def centred_shift(p, mu_prev, mu_next):
    prev = jnp.pad(p, ((0, 0), (1, 0), (0, 0)))[:, :-1]
    nxt = jnp.pad(p, ((0, 0), (0, 1), (0, 0)))[:, 1:]
    return p + mu_prev * (prev - p) + mu_next * (nxt - p)


def to_heads(t):
    return t.reshape(t.shape[:-1] + (N_HEADS_R, HEAD_R))


def rwkv_features(pr, base, rp):
    k = col(pr, base, OFF_RK, OFF_RV)
    v = col(pr, base, OFF_RV, OFF_RW)
    wd = jnp.tanh(col(pr, base, OFF_RW, OFF_RA))
    wd = wd.reshape(wd.shape[:-1] + (2, W_LORA))
    ad = col(pr, base, OFF_RA, OFF_MKV)
    ad = ad.reshape(ad.shape[:-1] + (2, A_LORA))
    w = -jax.nn.softplus(-(rp["w0"] + jnp.einsum("bldr,drc->bldc", wd, rp["w2"]))) - 0.5
    a = jax.nn.sigmoid(rp["a0"] + jnp.einsum("bldr,drc->bldc", ad, rp["a2"]))
    kk = to_heads(k * rp["kk"])
    kk = kk / jnp.maximum(jnp.sqrt(jnp.sum(kk * kk, -1, keepdims=True)), 1e-12)
    k_eff = k[:, :, None] * (1.0 + (a - 1.0) * rp["ka"])
    return {"decay": to_heads(jnp.exp(-jnp.exp(w))),
            "kk": kk,
            "kka": kk[:, :, None] * to_heads(a),
            "k": to_heads(k_eff),
            "v": to_heads(v)}


def wkv_scan(S0, f, d, r, reverse):
    tm = lambda t: jnp.moveaxis(t, 1, 0)
    xs = (tm(f["decay"][:, :, d]), tm(f["kk"]), tm(f["kka"][:, :, d]), tm(f["k"][:, :, d]),
          tm(f["v"]), None if r is None else tm(r))

    def step(S, inp):
        w_t, kk_t, kka_t, k_t, v_t, r_t = inp
        S = (S * w_t[:, :, None, :]
             - jnp.einsum("bhij,bhj->bhi", S, kk_t)[..., None] * kka_t[:, :, None, :]
             + v_t[..., None] * k_t[:, :, None, :])
        y = None if r_t is None else jnp.einsum("bhij,bhj->bhi", S, r_t)
        return S, y

    S, ys = lax.scan(step, S0, xs, reverse=reverse)
    return S, (None if ys is None else jnp.moveaxis(ys, 0, 1))


def rwkv_mixer(pr, base, S0_fwd, S0_bwd, with_out, rp):
    f = rwkv_features(pr, base, rp)
    r = to_heads(col(pr, base, OFF_RR, OFF_RG)) if with_out else None
    S_f, y_f = wkv_scan(S0_fwd, f, 0, r, reverse=False)
    S_b, y_b = wkv_scan(S0_bwd, f, 1, r, reverse=True)
    if not with_out:
        return S_f, S_b, None
    y = y_f + y_b
    mu = jnp.mean(y, -1, keepdims=True)
    var = jnp.mean(jnp.square(y - mu), -1, keepdims=True)
    yn = ((y - mu) * lax.rsqrt(var + GN_EPS)).reshape(y.shape[:2] + (C_R,)) * rp["ln_g"] + rp["ln_b"]
    bonus = jnp.sum(jnp.sum(r[:, :, None] * f["k"] * rp["rk"], -1, keepdims=True) * f["v"][:, :, None], axis=2)
    g = jax.nn.sigmoid(col(pr, base, OFF_RG, OFF_RK)) @ rp["g2"]
    return S_f, S_b, (yn + bonus.reshape(yn.shape)) * g


def mla_keys(p, base, mp, cos, sin):
    B, L = p.shape[:2]
    ckv = rmsnorm(col(p, base, OFF_MKV, OFF_MKR), mp["ckv_g"])
    kv = (ckv @ mp["w_ukv"]).reshape(B, L, N_HEADS_M, NOPE + V_HEAD)
    k_nope = rmsnorm(kv[..., :NOPE], mp["kn_g"])
    k_rope = rmsnorm(col(p, base, OFF_MKR, IN_COLS), mp["kr_g"])
    if cos is not None:
        k_rope = apply_rope2d(k_rope, cos, sin)
    return k_nope, k_rope, kv[..., NOPE:]


def mla_queries(p, base, mp, cos, sin):
    B, L = p.shape[:2]
    cq = rmsnorm(col(p, base, OFF_MQ, OFF_RR), mp["cq_g"])
    q = (cq @ mp["w_uq"]).reshape(B, L, N_HEADS_M, NOPE + ROPE)
    q_nope = rmsnorm(q[..., :NOPE], mp["qn_g"])
    q_rope = rmsnorm(q[..., NOPE:], mp["qr_g"])
    if cos is not None:
        q_rope = apply_rope2d(q_rope, cos[:, None], sin[:, None])
    return q_nope, q_rope


def attend(q_nope, q_rope, k_nope, k_rope, v):
    s = (jnp.einsum("bqhd,bkhd->bhqk", q_nope, k_nope)
         + jnp.einsum("bqhr,bkr->bhqk", q_rope, k_rope))
    pr = jax.nn.softmax(s.astype(jnp.float32) * ATTN_SCALE, axis=-1).astype(v.dtype)
    return jnp.einsum("bhqk,bkhd->bqhd", pr, v)


def latent_attention(q_nope, q_rope, k_nope, k_rope, v):
    B, L, H, _ = q_nope.shape
    nb = L // Q_BLOCK
    blk = lambda t: jnp.moveaxis(t.reshape((B, nb, Q_BLOCK) + t.shape[2:]), 1, 0)
    out = lax.map(lambda qs: attend(qs[0], qs[1], k_nope, k_rope, v), (blk(q_nope), blk(q_rope)))
    return jnp.moveaxis(out, 0, 1).reshape(B, L, H * V_HEAD)


def route(tok, w_router, router_bias):
    scores = jax.nn.sigmoid((tok @ w_router).astype(jnp.float32))
    biased = (scores + router_bias.astype(jnp.float32)).reshape(-1, N_GROUPS, EXPERTS_PER_GROUP)
    group_score = jnp.sum(lax.top_k(biased, 2)[0], -1)
    chosen = jnp.argmax(group_score, -1)
    in_group = jnp.arange(N_GROUPS)[None, :] == chosen[:, None]
    masked = jnp.where(in_group[:, :, None], biased, -jnp.inf).reshape(-1, N_EXPERTS)
    _, idx = lax.top_k(masked, TOP_K)
    w = jnp.take_along_axis(scores, idx, -1)
    return idx, w / jnp.sum(w, -1, keepdims=True)


def moe_ffn(tok, w_router, router_bias, w1, w3, w2):
    T, D = tok.shape
    idx, wts = route(tok, w_router, router_bias)
    M = T * TOP_K
    flat_e = idx.reshape(-1)
    flat_t = jnp.repeat(jnp.arange(T, dtype=jnp.int32), TOP_K)
    flat_w = wts.reshape(-1)
    order = jnp.argsort(flat_e)
    se, st, sw = flat_e[order], flat_t[order], flat_w[order]
    counts = jnp.bincount(flat_e, length=N_EXPERTS)
    starts = jnp.cumsum(counts) - counts
    pcounts = (counts + MOE_BLOCK - 1) // MOE_BLOCK * MOE_BLOCK
    pends = jnp.cumsum(pcounts)
    pstarts = pends - pcounts
    dest = pstarts[se] + jnp.arange(M) - starts[se]
    n_blocks = -(-M // MOE_BLOCK) + N_EXPERTS
    P = n_blocks * MOE_BLOCK
    row_tok = jnp.zeros((P,), jnp.int32).at[dest].set(st)
    row_w = jnp.zeros((P,), jnp.float32).at[dest].set(sw)
    block_e = jnp.clip(jnp.searchsorted(pends, jnp.arange(n_blocks) * MOE_BLOCK, side="right"),
                       0, N_EXPERTS - 1)

    def expert_block(args):
        e, toks = args
        xb = tok[toks]
        hb = jax.nn.silu(xb @ w1[e]) * (xb @ w3[e])
        return hb @ w2[e]

    yb = lax.map(expert_block, (block_e, row_tok.reshape(n_blocks, MOE_BLOCK))).reshape(P, D)
    return jax.ops.segment_sum(yb * row_w[:, None].astype(yb.dtype), row_tok, num_segments=T)


def setup_inputs(seed: int = 0) -> dict:
    key = jax.random.key(seed)
    ks = jax.random.split(key, 38)
    f32 = jnp.float32
    Lr, D = DEPTH, D_MODEL

    def nrm(i, shape, scale=1.0):
        return jax.random.normal(ks[i], shape, f32) * scale

    def gain(i, shape):
        return 1.0 + nrm(i, shape, 0.02)

    return {
        "x": nrm(0, (BATCH, SEQ, D)),
        "c": nrm(1, (BATCH, D)),
        "ctx": nrm(2, (BATCH, CTX_LEN, D)),
        "c_ctx": nrm(3, (D,)),
        "w_mod": nrm(4, (Lr, D, 6 * D), 0.5 * D ** -0.5),
        "b_mod": nrm(5, (Lr, 6 * D), 0.02),
        "norm1_g": gain(6, (Lr, D)),
        "norm2_g": gain(7, (Lr, D)),
        "w_in": nrm(8, (Lr, D, IN_COLS), D ** -0.5),
        "w_out": nrm(9, (Lr, MIX, D), MIX ** -0.5),
        "conv_dw": nrm(10, (Lr, CONV_W, C_CONV), CONV_W ** -0.5),
        "conv_b": nrm(11, (Lr, C_CONV), 0.02),
        "conv_ln_g": gain(12, (Lr, C_CONV)),
        "conv_ln_b": nrm(13, (Lr, C_CONV), 0.02),
        "r_mu": jax.random.uniform(ks[14], (Lr, 2, RWKV_COLS), f32, 0.0, 0.5),
        "r_w0": jax.random.uniform(ks[15], (Lr, 2, C_R), f32, -6.0, 1.0),
        "r_w2": nrm(16, (Lr, 2, W_LORA, C_R), 0.1 * W_LORA ** -0.5),
        "r_a0": nrm(17, (Lr, 2, C_R), 0.1),
        "r_a2": nrm(18, (Lr, 2, A_LORA, C_R), 0.1 * A_LORA ** -0.5),
        "r_g2": nrm(19, (Lr, G_LORA, C_R), G_LORA ** -0.5),
        "r_kk": 0.85 + nrm(20, (Lr, C_R), 0.02),
        "r_ka": gain(21, (Lr, C_R)),
        "r_rk": nrm(22, (Lr, N_HEADS_R, HEAD_R), 0.1),
        "r_ln_g": gain(23, (Lr, C_R)),
        "r_ln_b": nrm(24, (Lr, C_R), 0.02),
        "m_cq_g": gain(25, (Lr, Q_LORA)),
        "m_w_uq": nrm(26, (Lr, Q_LORA, N_HEADS_M * (NOPE + ROPE)), Q_LORA ** -0.5),
        "m_ckv_g": gain(27, (Lr, KV_LORA)),
        "m_w_ukv": nrm(28, (Lr, KV_LORA, N_HEADS_M * (NOPE + V_HEAD)), KV_LORA ** -0.5),
        "m_qn_g": gain(29, (Lr, NOPE)),
        "m_qr_g": gain(30, (Lr, ROPE)),
        "m_kn_g": gain(31, (Lr, NOPE)),
        "m_kr_g": gain(32, (Lr, ROPE)),
        "w_router": nrm(33, (D, N_EXPERTS), D ** -0.5),
        "router_bias": nrm(34, (N_EXPERTS,), 0.01),
        "moe_w1": nrm(35, (Lr, N_EXPERTS, D, D_EXPERT), D ** -0.5),
        "moe_w3": nrm(36, (Lr, N_EXPERTS, D, D_EXPERT), D ** -0.5),
        "moe_w2": nrm(37, (Lr, N_EXPERTS, D_EXPERT, D), D_EXPERT ** -0.5),
    }


def reference(x, c, ctx, c_ctx, w_mod, b_mod, norm1_g, norm2_g, w_in, w_out,
              conv_dw, conv_b, conv_ln_g, conv_ln_b,
              r_mu, r_w0, r_w2, r_a0, r_a2, r_g2, r_kk, r_ka, r_rk, r_ln_g, r_ln_b,
              m_cq_g, m_w_uq, m_ckv_g, m_w_ukv, m_qn_g, m_qr_g, m_kn_g, m_kr_g,
              w_router, router_bias, moe_w1, moe_w3, moe_w2):
    B, L, D = x.shape
    Lc = ctx.shape[1]
    rows = L // GRID_W
    cos, sin = rope2d_tables(rows)
    zero_state = jnp.zeros((B, N_HEADS_R, HEAD_R, HEAD_R), jnp.float32)
    h, hc = x, ctx
    for l in range(DEPTH):
        last = l == DEPTH - 1
        mod = jax.nn.silu(c) @ w_mod[l] + b_mod[l]
        sh1, sc1, ga1, sh2, sc2, ga2 = jnp.split(mod[:, None, :], 6, axis=-1)
        n_cm = 2 if last else 6
        cmods = jnp.split(jax.nn.silu(c_ctx) @ w_mod[l][:, :n_cm * D] + b_mod[l][:n_cm * D], n_cm)

        n = rmsnorm(h, norm1_g[l]) * (1.0 + sc1) + sh1
        nc = rmsnorm(hc, norm1_g[l]) * (1.0 + cmods[1]) + cmods[0]
        cbase = OFF_RK if last else OFF_CONV
        p = n @ w_in[l]
        pc = nc @ w_in[l][:, cbase:]

        conv_lat = conformer_conv(p[..., OFF_CONV:OFF_MQ], conv_dw[l], conv_b[l], conv_ln_g[l], conv_ln_b[l])

        rp = {"w0": r_w0[l], "w2": r_w2[l], "a0": r_a0[l], "a2": r_a2[l], "g2": r_g2[l],
              "kk": r_kk[l], "ka": r_ka[l], "rk": r_rk[l], "ln_g": r_ln_g[l], "ln_b": r_ln_b[l]}
        rbase = max(cbase, OFF_RR)
        pr_c = centred_shift(col(pc, cbase, rbase, OFF_MKV).astype(jnp.float32),
                             r_mu[l][0, rbase - OFF_RR:], r_mu[l][1, rbase - OFF_RR:])
        S_f, S_b, rw_ctx = rwkv_mixer(pr_c, rbase, zero_state, zero_state, not last, rp)
        pr = centred_shift(p[..., OFF_RR:OFF_MKV].astype(jnp.float32), r_mu[l][0], r_mu[l][1])
        _, _, rw_lat = rwkv_mixer(pr, OFF_RR, S_f, S_b, True, rp)

        mp = {"cq_g": m_cq_g[l], "w_uq": m_w_uq[l], "ckv_g": m_ckv_g[l], "w_ukv": m_w_ukv[l],
              "qn_g": m_qn_g[l], "qr_g": m_qr_g[l], "kn_g": m_kn_g[l], "kr_g": m_kr_g[l]}
        kn_c, kr_c, v_c = mla_keys(pc, cbase, mp, None, None)
        kn, kr, v = mla_keys(p, 0, mp, cos, sin)
        qn, qr = mla_queries(p, 0, mp, cos, sin)
        mla_lat = latent_attention(qn, qr, jnp.concatenate([kn_c, kn], 1),
                                   jnp.concatenate([kr_c, kr], 1), jnp.concatenate([v_c, v], 1))

        mix = jnp.concatenate([conv_lat, rw_lat.astype(h.dtype), mla_lat], -1) @ w_out[l]
        h = h + ga1 * mix
        n2 = rmsnorm(h, norm2_g[l]) * (1.0 + sc2) + sh2
        if last:
            h = h + ga2 * moe_ffn(n2.reshape(B * L, D), w_router, router_bias,
                                  moe_w1[l], moe_w3[l], moe_w2[l]).reshape(B, L, D)
        else:
            conv_ctx = conformer_conv(pc[..., OFF_CONV:OFF_MQ], conv_dw[l], conv_b[l], conv_ln_g[l], conv_ln_b[l])
            qn_c, qr_c = mla_queries(pc, 0, mp, None, None)
            mla_ctx = attend(qn_c, qr_c, kn_c, kr_c, v_c).reshape(B, Lc, C_M)
            mix_c = jnp.concatenate([conv_ctx, rw_ctx.astype(hc.dtype), mla_ctx], -1) @ w_out[l]
            hc = hc + cmods[2] * mix_c
            n2c = rmsnorm(hc, norm2_g[l]) * (1.0 + cmods[4]) + cmods[3]
            y = moe_ffn(jnp.concatenate([n2.reshape(B * L, D), n2c.reshape(B * Lc, D)], 0),
                        w_router, router_bias, moe_w1[l], moe_w3[l], moe_w2[l])
            h = h + ga2 * y[:B * L].reshape(B, L, D)
            hc = hc + cmods[5] * y[B * L:].reshape(B, Lc, D)
    return h
```

```python
import functools
import math

import jax
import jax.numpy as jnp
from jax import lax
from jax.experimental import pallas as pl
from jax.experimental.pallas import tpu as pltpu

GRID_W = 64
C_CONV = 1024
CONV_W = 31
N_HEADS_R = 16
HEAD_R = 64
C_R = N_HEADS_R * HEAD_R
W_LORA = 64
A_LORA = 64
G_LORA = 160
GN_EPS = 64e-5
N_HEADS_M = 16
NOPE = 128
ROPE = 64
V_HEAD = 128
Q_LORA = 1536
KV_LORA = 512
C_M = N_HEADS_M * V_HEAD
ROPE_BASE = 10000.0
ATTN_SCALE = (NOPE + ROPE) ** -0.5

OFF_CONV = 0
OFF_MQ = OFF_CONV + 2 * C_CONV
OFF_RR = OFF_MQ + Q_LORA
OFF_RG = OFF_RR + C_R
OFF_RK = OFF_RG + G_LORA
OFF_RV = OFF_RK + C_R
OFF_RW = OFF_RV + C_R
OFF_RA = OFF_RW + 2 * W_LORA
OFF_MKV = OFF_RA + 2 * A_LORA
OFF_MKR = OFF_MKV + KV_LORA
IN_COLS = OFF_MKR + ROPE

N_EXPERTS = 16
N_GROUPS = 4
EXPERTS_PER_GROUP = N_EXPERTS // N_GROUPS
TOP_K = 2

LANES = 128
SUBLANES = 8
VMEM_LIMIT = 56 * 1024 * 1024
CONV_HALO = 16
MOE_ROWS = 512

BF16 = jnp.bfloat16
F32 = jnp.float32


def _params(sem):
    return pltpu.CompilerParams(dimension_semantics=sem, vmem_limit_bytes=VMEM_LIMIT)


def _tile(n, target, mult):
    best = None
    for d in range(mult, min(n, target) + 1, mult):
        if n % d == 0:
            best = d
    return n if best is None else best


def _pad_cols(w, mult):
    pad = (-w.shape[-1]) % mult
    return w if pad == 0 else jnp.pad(w, ((0, 0), (0, pad)))


def _mod_body(c_ref, w_ref, b_ref, o_ref):
    cv = c_ref[...]
    a = (cv * jax.nn.sigmoid(cv)).astype(BF16)
    o_ref[...] = jnp.dot(a, w_ref[...].astype(BF16), preferred_element_type=F32) + b_ref[...]


def adaln_mod(rows, w, b):
    R, D = rows.shape
    N = w.shape[1]
    tn = _tile(N, 512, LANES)
    return pl.pallas_call(
        _mod_body,
        grid=(N // tn,),
        in_specs=[pl.BlockSpec((R, D), lambda j: (0, 0)),
                  pl.BlockSpec((D, tn), lambda j: (0, j)),
                  pl.BlockSpec((1, tn), lambda j: (0, j))],
        out_specs=pl.BlockSpec((R, tn), lambda j: (0, j)),
        out_shape=jax.ShapeDtypeStruct((R, N), F32),
        compiler_params=_params(("arbitrary",)),
        name="adaln_mod",
    )(rows, w, b.reshape(1, N))


def _norm_mod_body(x_ref, g_ref, sc_ref, sh_ref, o_ref):
    x = x_ref[...]
    y = x * lax.rsqrt(jnp.mean(x * x, axis=-1, keepdims=True) + 1e-6)
    o_ref[...] = (y * g_ref[...] * (1.0 + sc_ref[...]) + sh_ref[...]).astype(o_ref.dtype)


def norm_mod(h, g, sc, sh, rows_per_mod, out_dtype=BF16):
    T, D = h.shape
    R = sc.shape[0]
    tm = _tile(math.gcd(T, rows_per_mod), 256, SUBLANES)
    mod_row = lambda i: (jnp.minimum(i * tm // rows_per_mod, R - 1), 0, 0)
    return pl.pallas_call(
        _norm_mod_body,
        grid=(T // tm,),
        in_specs=[pl.BlockSpec((tm, D), lambda i: (i, 0)),
                  pl.BlockSpec((1, D), lambda i: (0, 0)),
                  pl.BlockSpec((None, 1, D), mod_row),
                  pl.BlockSpec((None, 1, D), mod_row)],
        out_specs=pl.BlockSpec((tm, D), lambda i: (i, 0)),
        out_shape=jax.ShapeDtypeStruct((T, D), out_dtype),
        compiler_params=_params(("parallel",)),
        name="norm_mod",
    )(h, g.reshape(1, D), sc.reshape(R, 1, D), sh.reshape(R, 1, D))


def _mm_body(a_ref, b_ref, o_ref, acc_ref):
    k = pl.program_id(2)

    @pl.when(k == 0)
    def _():
        acc_ref[...] = jnp.zeros_like(acc_ref)

    acc_ref[...] += jnp.dot(a_ref[...], b_ref[...], preferred_element_type=F32)

    @pl.when(k == pl.num_programs(2) - 1)
    def _():
        o_ref[...] = acc_ref[...].astype(o_ref.dtype)


def _mm_res_body(a_ref, b_ref, res_ref, gate_ref, o_ref, acc_ref):
    k = pl.program_id(2)

    @pl.when(k == 0)
    def _():
        acc_ref[...] = jnp.zeros_like(acc_ref)

    acc_ref[...] += jnp.dot(a_ref[...], b_ref[...], preferred_element_type=F32)

    @pl.when(k == pl.num_programs(2) - 1)
    def _():
        o_ref[...] = (res_ref[...] + gate_ref[...] * acc_ref[...]).astype(o_ref.dtype)


def matmul(a, b, out_dtype=F32, res=None, gate=None, rows_per_mod=None, name="matmul"):
    M, K = a.shape
    N = b.shape[1]
    tm = _tile(M if rows_per_mod is None else math.gcd(M, rows_per_mod), 1024, SUBLANES)
    tn = _tile(N, 1024, LANES)
    tk = _tile(K, 1024, LANES)
    grid = (M // tm, N // tn, K // tk)
    a_spec = pl.BlockSpec((tm, tk), lambda i, j, k: (i, k))
    b_spec = pl.BlockSpec((tk, tn), lambda i, j, k: (k, j))
    o_spec = pl.BlockSpec((tm, tn), lambda i, j, k: (i, j))
    scratch = [pltpu.VMEM((tm, tn), F32)]
    cp = _params(("parallel", "parallel", "arbitrary"))
    if res is None:
        return pl.pallas_call(_mm_body, grid=grid, in_specs=[a_spec, b_spec], out_specs=o_spec,
                              out_shape=jax.ShapeDtypeStruct((M, N), out_dtype),
                              scratch_shapes=scratch, compiler_params=cp, name=name)(a, b)
    R = gate.shape[0]
    gate_spec = pl.BlockSpec((None, 1, tn),
                             lambda i, j, k: (jnp.minimum(i * tm // rows_per_mod, R - 1), 0, j))
    return pl.pallas_call(_mm_res_body, grid=grid,
                          in_specs=[a_spec, b_spec, o_spec, gate_spec], out_specs=o_spec,
                          out_shape=jax.ShapeDtypeStruct((M, N), out_dtype),
                          scratch_shapes=scratch, compiler_params=cp, name=name)(
                              a, b, res, gate.reshape(R, 1, N))


def _conv_body(u_ref, up_ref, un_ref, dw_ref, db_ref, g_ref, b_ref, o_ref, buf_ref, sh_ref, *, tl, rc):
    i = pl.program_id(1)
    nl = pl.num_programs(1)

    def glu(u):
        return u[:, :C_CONV] * jax.nn.sigmoid(u[:, C_CONV:])

    buf_ref[pl.ds(0, CONV_HALO), :] = jnp.where(i > 0, glu(up_ref[...]), 0.0)
    buf_ref[pl.ds(CONV_HALO, tl), :] = glu(u_ref[...])
    buf_ref[pl.ds(CONV_HALO + tl, CONV_HALO), :] = jnp.where(i < nl - 1, glu(un_ref[...]), 0.0)
    nsh = tl + 2 * CONV_HALO - SUBLANES
    for s in range(1, SUBLANES):
        sh_ref[s - 1] = buf_ref[pl.ds(s, nsh), :]

    first = CONV_HALO - CONV_W // 2

    def chunk(c, carry):
        r0 = pl.multiple_of(c * rc, rc)
        acc = jnp.broadcast_to(db_ref[...], (rc, C_CONV))
        for k in range(CONV_W):
            q, s = divmod(first + k, SUBLANES)
            start = pl.multiple_of(r0 + q * SUBLANES, SUBLANES)
            win = buf_ref[pl.ds(start, rc), :] if s == 0 else sh_ref[s - 1, pl.ds(start, rc), :]
            acc = acc + dw_ref[pl.ds(k, 1), :] * win
        mu = jnp.mean(acc, axis=-1, keepdims=True)
        d = acc - mu
        var = jnp.mean(d * d, axis=-1, keepdims=True)
        y = d * lax.rsqrt(var + 1e-5) * g_ref[...] + b_ref[...]
        o_ref[pl.ds(r0, rc), :] = (y * jax.nn.sigmoid(y)).astype(o_ref.dtype)
        return carry

    lax.fori_loop(0, tl // rc, chunk, 0)


def conformer_conv(u, dw, db, ln_g, ln_b, out_dtype=BF16):
    B, L, _ = u.shape
    tl = _tile(L, 512, CONV_HALO)
    rc = _tile(tl, 32, SUBLANES)
    hb = tl // CONV_HALO
    nhalo = L // CONV_HALO
    body = functools.partial(_conv_body, tl=tl, rc=rc)
    vec = lambda a: a.reshape(1, C_CONV)
    vspec = pl.BlockSpec((1, C_CONV), lambda b, i: (0, 0))
    return pl.pallas_call(
        body,
        grid=(B, L // tl),
        in_specs=[pl.BlockSpec((None, tl, 2 * C_CONV), lambda b, i: (b, i, 0)),
                  pl.BlockSpec((None, CONV_HALO, 2 * C_CONV),
                               lambda b, i: (b, jnp.maximum(i * hb - 1, 0), 0)),
                  pl.BlockSpec((None, CONV_HALO, 2 * C_CONV),
                               lambda b, i: (b, jnp.minimum((i + 1) * hb, nhalo - 1), 0)),
                  pl.BlockSpec((CONV_W, C_CONV), lambda b, i: (0, 0)),
                  vspec, vspec, vspec],
        out_specs=pl.BlockSpec((None, tl, C_CONV), lambda b, i: (b, i, 0)),
        out_shape=jax.ShapeDtypeStruct((B, L, C_CONV), out_dtype),
        scratch_shapes=[pltpu.VMEM((tl + 2 * CONV_HALO, C_CONV), F32),
                        pltpu.VMEM((SUBLANES - 1, tl + 2 * CONV_HALO - SUBLANES, C_CONV), F32)],
        compiler_params=_params(("parallel", "arbitrary")),
        name="conformer_conv",
    )(u, u, u, dw, vec(db), vec(ln_g), vec(ln_b))


def _scan_body(w_ref, kk_ref, kka_ref, k_ref, v_ref, r_ref, y_ref, s_ref, *, tc, n):
    @pl.when(pl.program_id(0) == 0)
    def _():
        s_ref[...] = jnp.zeros_like(s_ref)

    def step(t, carry):
        w = w_ref[t]
        kk = kk_ref[t]
        kka = kka_ref[t]
        k = k_ref[t]
        r = r_ref[t]

        def rows(ib, c2):
            ys = []
            for ii in range(SUBLANES):
                i = ib * SUBLANES + ii
                s = s_ref[i]
                sa = jnp.sum(s * kk, axis=0, keepdims=True)
                vi = v_ref[t, pl.ds(i, 1), :]
                s_new = s * w - sa * kka + vi * k
                s_ref[i] = s_new
                ys.append(jnp.sum(s_new * r, axis=0, keepdims=True))
            y_ref[t, pl.ds(pl.multiple_of(ib * SUBLANES, SUBLANES), SUBLANES), :] = jnp.concatenate(ys, axis=0)
            return c2

        lax.fori_loop(0, n // SUBLANES, rows, 0)
        return carry

    lax.fori_loop(0, tc, step, 0)


def wkv_scan_lanes(w, kk, kka, k, v, r):
    T, n, lanes = w.shape
    tc = _tile(T, 32, 1)
    spec = pl.BlockSpec((tc, n, lanes), lambda g: (g, 0, 0))
    return pl.pallas_call(
        functools.partial(_scan_body, tc=tc, n=n),
        grid=(T // tc,),
        in_specs=[spec] * 6,
        out_specs=spec,
        out_shape=jax.ShapeDtypeStruct((T, n, lanes), F32),
        scratch_shapes=[pltpu.VMEM((n, n, lanes), F32)],
        compiler_params=_params(("arbitrary",)),
        name="wkv_scan",
    )(w, kk, kka, k, v, r)


def _attn_body(q_ref, k_ref, v_ref, o_ref):
    s = lax.dot_general(q_ref[...], k_ref[...], (((1,), (1,)), ((), ())), preferred_element_type=F32)
    m = jnp.max(s, axis=-1, keepdims=True)
    p = jnp.exp(s - m)
    l = jnp.sum(p, axis=-1, keepdims=True)
    o = jnp.dot(p.astype(BF16), v_ref[...], preferred_element_type=F32)
    o_ref[...] = (o / l).astype(o_ref.dtype)


def mla_attention(q, k, v, out_dtype=BF16):
    B, H, Lq, dk = q.shape
    Lk, dv = v.shape[2], v.shape[3]
    tq = _tile(Lq, 256, SUBLANES)
    return pl.pallas_call(
        _attn_body,
        grid=(B, H, Lq // tq),
        in_specs=[pl.BlockSpec((None, None, tq, dk), lambda b, h, i: (b, h, i, 0)),
                  pl.BlockSpec((None, None, Lk, dk), lambda b, h, i: (b, h, 0, 0)),
                  pl.BlockSpec((None, None, Lk, dv), lambda b, h, i: (b, h, 0, 0))],
        out_specs=pl.BlockSpec((None, tq, dv), lambda b, h, i: (b, i, h)),
        out_shape=jax.ShapeDtypeStruct((B, Lq, H * dv), out_dtype),
        compiler_params=_params(("parallel", "parallel", "arbitrary")),
        name="mla_attention",
    )(q, k, v)


def _moe_body(be_ref, nu_ref, x_ref, w1_ref, w3_ref, w2_ref, o_ref):
    blk = pl.program_id(0)
    f = pl.program_id(1)

    @pl.when(f == 0)
    def _():
        o_ref[...] = jnp.zeros_like(o_ref)

    @pl.when(blk < nu_ref[0])
    def _():
        x = x_ref[...]
        h1 = jnp.dot(x, w1_ref[...], preferred_element_type=F32)
        h3 = jnp.dot(x, w3_ref[...], preferred_element_type=F32)
        hb = (h1 * jax.nn.sigmoid(h1) * h3).astype(BF16)
        o_ref[...] += jnp.dot(hb, w2_ref[...], preferred_element_type=F32)


def moe_experts(xs, block_e, n_used, w1, w3, w2):
    P, D = xs.shape
    Fd = w1.shape[2]
    tf = _tile(Fd, 256, LANES)
    nf = Fd // tf
    nb = P // MOE_ROWS

    def fidx(i, f, nu):
        return jnp.where(i < nu[0], f, nf - 1)

    grid_spec = pltpu.PrefetchScalarGridSpec(
        num_scalar_prefetch=2,
        grid=(nb, nf),
        in_specs=[pl.BlockSpec((MOE_ROWS, D), lambda i, f, be, nu: (i, 0)),
                  pl.BlockSpec((None, D, tf), lambda i, f, be, nu: (be[i], 0, fidx(i, f, nu))),
                  pl.BlockSpec((None, D, tf), lambda i, f, be, nu: (be[i], 0, fidx(i, f, nu))),
                  pl.BlockSpec((None, tf, D), lambda i, f, be, nu: (be[i], fidx(i, f, nu), 0))],
        out_specs=pl.BlockSpec((MOE_ROWS, D), lambda i, f, be, nu: (i, 0)),
    )
    return pl.pallas_call(
        _moe_body,
        grid_spec=grid_spec,
        out_shape=jax.ShapeDtypeStruct((P, D), F32),
        compiler_params=_params(("arbitrary", "arbitrary")),
        name="moe_experts",
    )(block_e, n_used, xs, w1, w3, w2)


def _route(tok, w_router, router_bias):
    logits = jnp.dot(tok, w_router, precision=lax.Precision.HIGHEST)
    scores = jax.nn.sigmoid(logits.astype(F32))
    biased = (scores + router_bias.astype(F32)).reshape(-1, N_GROUPS, EXPERTS_PER_GROUP)
    group_score = jnp.sum(lax.top_k(biased, 2)[0], -1)
    chosen = jnp.argmax(group_score, -1)
    in_group = jnp.arange(N_GROUPS)[None, :] == chosen[:, None]
    masked = jnp.where(in_group[:, :, None], biased, -jnp.inf).reshape(-1, N_EXPERTS)
    _, idx = lax.top_k(masked, TOP_K)
    w = jnp.take_along_axis(scores, idx, -1)
    return idx, w / jnp.sum(w, -1, keepdims=True)


def moe_ffn(tok, w_router, router_bias, w1, w3, w2):
    T, D = tok.shape
    idx, wts = _route(tok, w_router, router_bias)
    M = T * TOP_K
    flat_e = idx.reshape(-1).astype(jnp.int32)
    onehot = (flat_e[:, None] == jnp.arange(N_EXPERTS, dtype=jnp.int32)[None, :]).astype(jnp.int32)
    csum = jnp.cumsum(onehot, axis=0)
    rank = jnp.take_along_axis(csum, flat_e[:, None], axis=1)[:, 0] - 1
    counts = csum[-1]
    pcounts = (counts + MOE_ROWS - 1) // MOE_ROWS * MOE_ROWS
    pends = jnp.cumsum(pcounts)
    pstarts = pends - pcounts
    dest = pstarts[flat_e] + rank
    n_blocks = -(-M // MOE_ROWS) + N_EXPERTS
    P = n_blocks * MOE_ROWS
    flat_t = jnp.repeat(jnp.arange(T, dtype=jnp.int32), TOP_K)
    row_tok = jnp.zeros((P,), jnp.int32).at[dest].set(flat_t)
    n_used = (pends[-1] // MOE_ROWS).astype(jnp.int32)
    blk_start = jnp.arange(n_blocks, dtype=jnp.int32) * MOE_ROWS
    block_e = jnp.clip(jnp.searchsorted(pends, blk_start, side="right"), 0, N_EXPERTS - 1).astype(jnp.int32)
    last_e = block_e[jnp.maximum(n_used - 1, 0)]
    block_e = jnp.where(jnp.arange(n_blocks) < n_used, block_e, last_e)
    xs = tok.astype(BF16)[row_tok]
    yb = moe_experts(xs, block_e, n_used.reshape(1), w1, w3, w2)
    pair = yb[dest].reshape(T, TOP_K, D)
    return jnp.sum(pair * wts[:, :, None], axis=1)


def _rmsnorm(x, g, eps=1e-6):
    return x * lax.rsqrt(jnp.mean(x * x, -1, keepdims=True) + eps) * g


def _rope_tables(rows):
    half = ROPE // 2
    inv_freq = ROPE_BASE ** (-jnp.arange(0, half, 2, dtype=F32) / half)
    r = jnp.repeat(jnp.arange(rows, dtype=F32), GRID_W)
    cl = jnp.tile(jnp.arange(GRID_W, dtype=F32), rows)
    ar = r[:, None] * inv_freq
    ac = cl[:, None] * inv_freq
    ang = jnp.concatenate([ar, ar, ac, ac], -1)
    return jnp.cos(ang), jnp.sin(ang)


def _rope(x, cos, sin):
    x1, x2, x3, x4 = jnp.split(x, 4, axis=-1)
    rot = jnp.concatenate([-x2, x1, -x4, x3], -1)
    return x * cos + rot * sin


def _shift(p, mu_prev, mu_next):
    prev = jnp.pad(p, ((0, 0), (1, 0), (0, 0)))[:, :-1]
    nxt = jnp.pad(p, ((0, 0), (0, 1), (0, 0)))[:, 1:]
    return p + mu_prev * (prev - p) + mu_next * (nxt - p)


def _heads(t):
    return t.reshape(t.shape[:-1] + (N_HEADS_R, HEAD_R))


def rwkv_group(pr, Lc, rp):
    B, Lt, _ = pr.shape
    c = lambda lo, hi: pr[..., lo - OFF_RR:hi - OFF_RR]
    r = _heads(c(OFF_RR, OFF_RG))
    k = c(OFF_RK, OFF_RV)
    v = _heads(c(OFF_RV, OFF_RW))
    wd = jnp.tanh(c(OFF_RW, OFF_RA)).reshape(B, Lt, 2, W_LORA)
    ad = c(OFF_RA, OFF_MKV).reshape(B, Lt, 2, A_LORA)
    hp = lax.Precision.HIGHEST
    w = -jax.nn.softplus(-(rp["w0"] + jnp.einsum("bldr,drc->bldc", wd, rp["w2"], precision=hp))) - 0.5
    a = jax.nn.sigmoid(rp["a0"] + jnp.einsum("bldr,drc->bldc", ad, rp["a2"], precision=hp))
    kk = _heads(k * rp["kk"])
    kk = kk / jnp.maximum(jnp.sqrt(jnp.sum(kk * kk, -1, keepdims=True)), 1e-12)
    k_eff = _heads(k[:, :, None] * (1.0 + (a - 1.0) * rp["ka"]))
    decay = _heads(jnp.exp(-jnp.exp(w)))
    kka = kk[:, :, None] * _heads(a)

    rev = jnp.concatenate([jnp.arange(Lc - 1, -1, -1), jnp.arange(Lt - 1, Lc - 1, -1)])

    def lanes2(t):
        f = jnp.transpose(t[:, :, 0], (1, 3, 0, 2)).reshape(Lt, HEAD_R, B * N_HEADS_R)
        b = jnp.transpose(t[:, :, 1], (1, 3, 0, 2)).reshape(Lt, HEAD_R, B * N_HEADS_R)[rev]
        return jnp.concatenate([f, b], -1)

    def lanes1(t):
        f = jnp.transpose(t, (1, 3, 0, 2)).reshape(Lt, HEAD_R, B * N_HEADS_R)
        return jnp.concatenate([f, f[rev]], -1)

    y = wkv_scan_lanes(lanes2(decay), lanes1(kk), lanes2(kka), lanes2(k_eff), lanes1(v), lanes1(r))
    nl = B * N_HEADS_R
    y = y[:, :, :nl] + y[rev][:, :, nl:]
    y = jnp.transpose(y.reshape(Lt, HEAD_R, B, N_HEADS_R), (2, 0, 3, 1))
    mu = jnp.mean(y, -1, keepdims=True)
    var = jnp.mean(jnp.square(y - mu), -1, keepdims=True)
    yn = ((y - mu) * lax.rsqrt(var + GN_EPS)).reshape(B, Lt, C_R) * rp["ln_g"] + rp["ln_b"]
    bonus = jnp.sum(jnp.sum(r[:, :, None] * k_eff * rp["rk"], -1, keepdims=True) * v[:, :, None], axis=2)
    g = jnp.dot(jax.nn.sigmoid(c(OFF_RG, OFF_RK)), rp["g2"], precision=hp)
    return (yn + bonus.reshape(B, Lt, C_R)) * g


def mla_heads_kv(ckv_raw, kr_raw, mp, cos, sin):
    B, L, _ = ckv_raw.shape
    ckv = _rmsnorm(ckv_raw, mp["ckv_g"]).astype(BF16).reshape(B * L, KV_LORA)
    kv = matmul(ckv, mp["w_ukv"], name="mla_kv_up").reshape(B, L, N_HEADS_M, NOPE + V_HEAD)
    k_nope = _rmsnorm(kv[..., :NOPE], mp["kn_g"])
    k_rope = _rmsnorm(kr_raw, mp["kr_g"])
    if cos is not None:
        k_rope = _rope(k_rope, cos, sin)
    k_rope = jnp.broadcast_to(k_rope[:, :, None, :], (B, L, N_HEADS_M, ROPE))
    k_cat = jnp.concatenate([k_nope, k_rope], -1).astype(BF16)
    return jnp.transpose(k_cat, (0, 2, 1, 3)), jnp.transpose(kv[..., NOPE:].astype(BF16), (0, 2, 1, 3))


def mla_heads_q(cq_raw, mp, cos, sin):
    B, L, _ = cq_raw.shape
    cq = _rmsnorm(cq_raw, mp["cq_g"]).astype(BF16).reshape(B * L, Q_LORA)
    q = matmul(cq, mp["w_uq"], name="mla_q_up").reshape(B, L, N_HEADS_M, NOPE + ROPE)
    q_nope = _rmsnorm(q[..., :NOPE], mp["qn_g"])
    q_rope = _rmsnorm(q[..., NOPE:], mp["qr_g"])
    if cos is not None:
        q_rope = _rope(q_rope, cos[:, None], sin[:, None])
    q_cat = (jnp.concatenate([q_nope, q_rope], -1) * ATTN_SCALE).astype(BF16)
    return jnp.transpose(q_cat, (0, 2, 1, 3))


def kernel(x, c, ctx, c_ctx, w_mod, b_mod, norm1_g, norm2_g, w_in, w_out, conv_dw, conv_b, conv_ln_g, conv_ln_b, r_mu, r_w0, r_w2, r_a0, r_a2, r_g2, r_kk, r_ka, r_rk, r_ln_g, r_ln_b, m_cq_g, m_w_uq, m_ckv_g, m_w_ukv, m_qn_g, m_qr_g, m_kn_g, m_kr_g, w_router, router_bias, moe_w1, moe_w3, moe_w2):
    B, L, D = x.shape
    Lc = ctx.shape[1]
    depth = w_mod.shape[0]
    nlat, nctx = B * L, B * Lc
    cos, sin = _rope_tables(L // GRID_W)

    H = jnp.concatenate([x.reshape(nlat, D), ctx.reshape(nctx, D)], 0)
    crow = jnp.concatenate([c, c_ctx[None, :], jnp.zeros((SUBLANES - (B + 1) % SUBLANES, D), F32)], 0)

    for l in range(depth):
        last = l == depth - 1
        mod = adaln_mod(crow, w_mod[l], b_mod[l])[:B + 1]
        sh1, sc1, ga1, sh2, sc2, ga2 = jnp.split(mod, 6, axis=-1)

        n = norm_mod(H, norm1_g[l], sc1, sh1, L)
        wi = w_in[l].astype(BF16)
        u = matmul(n, wi[:, OFF_CONV:OFF_MQ], name="in_conv")
        cq_raw = matmul(n, wi[:, OFF_MQ:OFF_RR], name="in_mq")
        pr_raw = matmul(n, _pad_cols(wi[:, OFF_RR:OFF_MKV], LANES), name="in_rwkv")[:, :OFF_MKV - OFF_RR]
        pkv = matmul(n, _pad_cols(wi[:, OFF_MKV:IN_COLS], LANES), name="in_mkv")
        lat = lambda t: t[:nlat].reshape(B, L, t.shape[-1])
        cx = lambda t: t[nlat:].reshape(B, Lc, t.shape[-1])

        conv_lat = conformer_conv(lat(u), conv_dw[l], conv_b[l], conv_ln_g[l], conv_ln_b[l])
        rp = {"w0": r_w0[l], "w2": r_w2[l], "a0": r_a0[l], "a2": r_a2[l], "g2": r_g2[l],
              "kk": r_kk[l], "ka": r_ka[l], "rk": r_rk[l], "ln_g": r_ln_g[l], "ln_b": r_ln_b[l]}
        pr = jnp.concatenate([_shift(cx(pr_raw), r_mu[l][0], r_mu[l][1]),
                              _shift(lat(pr_raw), r_mu[l][0], r_mu[l][1])], 1)
        rw = rwkv_group(pr, Lc, rp)
        mp = {"cq_g": m_cq_g[l], "w_uq": m_w_uq[l].astype(BF16), "ckv_g": m_ckv_g[l],
              "w_ukv": m_w_ukv[l].astype(BF16), "qn_g": m_qn_g[l], "qr_g": m_qr_g[l],
              "kn_g": m_kn_g[l], "kr_g": m_kr_g[l]}
        kc_c, v_c = mla_heads_kv(cx(pkv)[..., :KV_LORA], cx(pkv)[..., KV_LORA:KV_LORA + ROPE], mp, None, None)
        kc_l, v_l = mla_heads_kv(lat(pkv)[..., :KV_LORA], lat(pkv)[..., KV_LORA:KV_LORA + ROPE], mp, cos, sin)
        q_l = mla_heads_q(lat(cq_raw), mp, cos, sin)
        mla_lat = mla_attention(q_l, jnp.concatenate([kc_c, kc_l], 2), jnp.concatenate([v_c, v_l], 2))

        wo = w_out[l].astype(BF16)
        mix_lat = jnp.concatenate([conv_lat.reshape(nlat, C_CONV), rw[:, Lc:].reshape(nlat, C_R).astype(BF16),
                                   mla_lat.reshape(nlat, C_M)], -1)
        w1, w3, w2 = moe_w1[l].astype(BF16), moe_w3[l].astype(BF16), moe_w2[l].astype(BF16)
        if last:
            Hl = matmul(mix_lat, wo, res=H[:nlat], gate=ga1[:B], rows_per_mod=L, name="out_proj")
            n2 = norm_mod(Hl, norm2_g[l], sc2[:B], sh2[:B], L, out_dtype=F32)
            y = moe_ffn(n2, w_router, router_bias, w1, w3, w2)
            H = Hl + jnp.repeat(ga2[:B], L, axis=0) * y
        else:
            conv_ctx = conformer_conv(cx(u), conv_dw[l], conv_b[l], conv_ln_g[l], conv_ln_b[l])
            q_c = mla_heads_q(cx(cq_raw), mp, None, None)
            mla_ctx = mla_attention(q_c, kc_c, v_c)
            mix_ctx = jnp.concatenate([conv_ctx.reshape(nctx, C_CONV), rw[:, :Lc].reshape(nctx, C_R).astype(BF16),
                                       mla_ctx.reshape(nctx, C_M)], -1)
            mix = jnp.concatenate([mix_lat, mix_ctx], 0)
            H = matmul(mix, wo, res=H, gate=ga1, rows_per_mod=L, name="out_proj")
            n2 = norm_mod(H, norm2_g[l], sc2, sh2, L, out_dtype=F32)
            y = moe_ffn(n2, w_router, router_bias, w1, w3, w2)
            grow = jnp.concatenate([jnp.repeat(ga2[:B], L, axis=0), jnp.repeat(ga2[B:], nctx, axis=0)], 0)
            H = H + grow * y
    return H[:nlat].reshape(B, L, D)
```

```python
import functools
import math

import jax
import jax.numpy as jnp
from jax import lax
from jax.experimental import pallas as pl
from jax.experimental.pallas import tpu as pltpu

GRID_W = 64
C_CONV = 1024
CONV_W = 31
N_HEADS_R = 16
HEAD_R = 64
C_R = N_HEADS_R * HEAD_R
W_LORA = 64
A_LORA = 64
G_LORA = 160
GN_EPS = 64e-5
N_HEADS_M = 16
NOPE = 128
ROPE = 64
V_HEAD = 128
Q_LORA = 1536
KV_LORA = 512
C_M = N_HEADS_M * V_HEAD
ROPE_BASE = 10000.0
ATTN_SCALE = (NOPE + ROPE) ** -0.5
RMS_EPS = 1e-6
LN_EPS = 1e-5

OFF_CONV = 0
OFF_MQ = OFF_CONV + 2 * C_CONV
OFF_RR = OFF_MQ + Q_LORA
OFF_RG = OFF_RR + C_R
OFF_RK = OFF_RG + G_LORA
OFF_RV = OFF_RK + C_R
OFF_RW = OFF_RV + C_R
OFF_RA = OFF_RW + 2 * W_LORA
OFF_MKV = OFF_RA + 2 * A_LORA
OFF_MKR = OFF_MKV + KV_LORA
IN_COLS = OFF_MKR + ROPE

N_EXPERTS = 16
N_GROUPS = 4
EXPERTS_PER_GROUP = N_EXPERTS // N_GROUPS
TOP_K = 2

LANES = 128
SUBLANES = 8
VMEM_LIMIT = 56 * 1024 * 1024
CONV_HALO = 16
MOE_ROWS = 512

RC_R, RC_K, RC_V, RC_W, RC_A, RC_G = 0, C_R, 2 * C_R, 3 * C_R, 3 * C_R + 2 * W_LORA, 3 * C_R + 2 * W_LORA + 2 * A_LORA
G_PAD = 2 * LANES
RCOLS = RC_G + G_PAD
QH = NOPE + 2 * ROPE
KD = NOPE + ROPE

BF16 = jnp.bfloat16
F32 = jnp.float32
HP = lax.Precision.HIGHEST


def _params(sem):
    return pltpu.CompilerParams(dimension_semantics=sem, vmem_limit_bytes=VMEM_LIMIT)


def _tile(n, target, mult):
    best = None
    for d in range(mult, min(n, target) + 1, mult):
        if n % d == 0:
            best = d
    return n if best is None else best


def _silu(x):
    return x * jax.nn.sigmoid(x)


def _rms(x, g):
    return x * lax.rsqrt(jnp.mean(x * x, axis=-1, keepdims=True) + RMS_EPS) * g


def _cast_body(x_ref, o_ref):
    o_ref[...] = x_ref[...].astype(o_ref.dtype)


def cast_bf16(w):
    shape = w.shape
    w2 = w.reshape(-1, shape[-1])
    R, C = w2.shape
    tr = _tile(R, 512, 2 * SUBLANES)
    tc = _tile(C, 2048, LANES)
    out = pl.pallas_call(
        _cast_body, grid=(R // tr, C // tc),
        in_specs=[pl.BlockSpec((tr, tc), lambda i, j: (i, j))],
        out_specs=pl.BlockSpec((tr, tc), lambda i, j: (i, j)),
        out_shape=jax.ShapeDtypeStruct((R, C), BF16),
        compiler_params=_params(("parallel", "parallel")), name="cast_bf16")(w2)
    return out.reshape(shape)


def _mod_body(c_ref, w_ref, b_ref, o_ref):
    a = _silu(c_ref[...]).astype(BF16)
    o_ref[...] = jnp.dot(a, w_ref[...].astype(BF16), preferred_element_type=F32) + b_ref[...]


def adaln_mod(rows, w, b):
    R, D = rows.shape
    N = w.shape[1]
    tn = _tile(N, 512, LANES)
    return pl.pallas_call(
        _mod_body,
        grid=(N // tn,),
        in_specs=[pl.BlockSpec((R, D), lambda j: (0, 0)),
                  pl.BlockSpec((D, tn), lambda j: (0, j)),
                  pl.BlockSpec((1, tn), lambda j: (0, j))],
        out_specs=pl.BlockSpec((R, tn), lambda j: (0, j)),
        out_shape=jax.ShapeDtypeStruct((R, N), F32),
        compiler_params=_params(("arbitrary",)),
        name="adaln_mod",
    )(rows, w, b.reshape(1, N))


def _norm_mod_body(x_ref, g_ref, sc_ref, sh_ref, o_ref):
    o_ref[...] = (_rms(x_ref[...], g_ref[...]) * (1.0 + sc_ref[...]) + sh_ref[...]).astype(o_ref.dtype)


def norm_mod(h, g, sc, sh, rows_per_mod, out_dtype=BF16):
    T, D = h.shape
    R = sc.shape[0]
    tm = _tile(math.gcd(T, rows_per_mod), 256, 2 * SUBLANES)
    mod_row = lambda i: (jnp.minimum(i * tm // rows_per_mod, R - 1), 0, 0)
    return pl.pallas_call(
        _norm_mod_body,
        grid=(T // tm,),
        in_specs=[pl.BlockSpec((tm, D), lambda i: (i, 0)),
                  pl.BlockSpec((1, D), lambda i: (0, 0)),
                  pl.BlockSpec((None, 1, D), mod_row),
                  pl.BlockSpec((None, 1, D), mod_row)],
        out_specs=pl.BlockSpec((tm, D), lambda i: (i, 0)),
        out_shape=jax.ShapeDtypeStruct((T, D), out_dtype),
        compiler_params=_params(("parallel",)),
        name="norm_mod",
    )(h, g.reshape(1, D), sc.reshape(R, 1, D), sh.reshape(R, 1, D))


def _mm_body(*refs, n_extra, n_out, epi):
    a_ref, b_ref = refs[0], refs[1]
    extra = refs[2:2 + n_extra]
    outs = refs[2 + n_extra:2 + n_extra + n_out]
    acc_ref = refs[-1]
    k = pl.program_id(2)

    @pl.when(k == 0)
    def _():
        acc_ref[...] = jnp.zeros_like(acc_ref)

    acc_ref[...] += jnp.dot(a_ref[...], b_ref[...], preferred_element_type=F32)

    @pl.when(k == pl.num_programs(2) - 1)
    def _():
        epi(acc_ref[...], extra, outs)


def matmul(a, b, *, epi, out_shapes, out_specs, tm, tn, tk=None, extra=(), extra_specs=(),
           n_row_blocks=None, row_block_off=0, name="matmul"):
    M, K = a.shape
    N = b.shape[1]
    tk = _tile(K, 1024, LANES) if tk is None else tk
    nrb = M // tm if n_row_blocks is None else n_row_blocks
    grid = (nrb, N // tn, K // tk)
    return pl.pallas_call(
        functools.partial(_mm_body, n_extra=len(extra), n_out=len(out_shapes), epi=epi),
        grid=grid,
        in_specs=[pl.BlockSpec((tm, tk), lambda i, j, k: (i + row_block_off, k)),
                  pl.BlockSpec((tk, tn), lambda i, j, k: (k, j))] + list(extra_specs),
        out_specs=list(out_specs), out_shape=list(out_shapes),
        scratch_shapes=[pltpu.VMEM((tm, tn), F32)],
        compiler_params=_params(("parallel", "parallel", "arbitrary")), name=name)(a, b, *extra)


def _epi_plain(acc, extra, outs):
    outs[0][...] = acc.astype(outs[0].dtype)


def matmul_plain(a, b, out_dtype=F32, name="matmul"):
    M, N = a.shape[0], b.shape[1]
    tm = _tile(M, 1024, 2 * SUBLANES)
    tn = _tile(N, 1024, LANES)
    return matmul(a, b, epi=_epi_plain, out_shapes=[jax.ShapeDtypeStruct((M, N), out_dtype)],
                  out_specs=[pl.BlockSpec((tm, tn), lambda i, j, k: (i, j))], tm=tm, tn=tn, name=name)[0]


def _rope_mix(t, tab):
    lane = lax.broadcasted_iota(jnp.int32, t.shape, 1)
    ss = jnp.sum(jnp.where(lane < ROPE, t * t, 0.0), axis=-1, keepdims=True)
    m = t * tab
    return (m + pltpu.roll(m, ROPE, axis=1)) * lax.rsqrt(ss * (1.0 / ROPE) + RMS_EPS)


def _epi_mq(acc, extra, outs):
    outs[0][...] = _rms(acc, extra[0][...]).astype(outs[0].dtype)


def _epi_mkv(acc, extra, outs):
    g_ref, tab_ref = extra
    outs[0][...] = _rms(acc[:, :KV_LORA], g_ref[...]).astype(outs[0].dtype)
    outs[1][...] = _rope_mix(acc[:, KV_LORA:], tab_ref[...])[:, :ROPE]


def _epi_qup(acc, extra, outs):
    g_ref, tab_ref = extra
    o = outs[0]
    o[:, :NOPE] = (_rms(acc[:, :NOPE], g_ref[...]) * ATTN_SCALE).astype(o.dtype)
    o[:, NOPE:] = (_rope_mix(acc[:, NOPE:], tab_ref[...])[:, :ROPE] * ATTN_SCALE).astype(o.dtype)


def _epi_kvup(acc, extra, outs):
    g_ref, kr_ref = extra
    ok, ov = outs
    ok[:, :NOPE] = _rms(acc[:, :NOPE], g_ref[...]).astype(ok.dtype)
    ok[:, NOPE:] = kr_ref[...].astype(ok.dtype)
    ov[...] = acc[:, NOPE:].astype(ov.dtype)


def _epi_res(acc, extra, outs):
    res_ref, gate_ref = extra
    outs[0][...] = res_ref[...] + gate_ref[...] * acc


def mla_q_up(cq, w, qn_g, tab, B, Lx, row_off):
    tm = _tile(Lx, 512, 2 * SUBLANES)
    nt = Lx // tm
    return matmul(
        cq, w, epi=_epi_qup, tm=tm, tn=QH, tk=Q_LORA,
        extra=(qn_g.reshape(1, NOPE), tab),
        extra_specs=(pl.BlockSpec((1, NOPE), lambda i, j, k: (0, 0)),
                     pl.BlockSpec((tm, 2 * ROPE), lambda i, j, k: (i % nt, 0))),
        out_shapes=[jax.ShapeDtypeStruct((B, N_HEADS_M, Lx, KD), BF16)],
        out_specs=[pl.BlockSpec((None, None, tm, KD), lambda i, j, k: (i // nt, j, i % nt, 0))],
        n_row_blocks=B * nt, row_block_off=row_off // tm, name="mla_q_up")[0]


def mla_kv_up(ckv, w, kn_g, kr, B, Lx, row_off):
    tm = _tile(Lx, 512, 2 * SUBLANES)
    nt = Lx // tm
    rbo = row_off // tm
    return matmul(
        ckv, w, epi=_epi_kvup, tm=tm, tn=NOPE + V_HEAD, tk=KV_LORA,
        extra=(kn_g.reshape(1, NOPE), kr),
        extra_specs=(pl.BlockSpec((1, NOPE), lambda i, j, k: (0, 0)),
                     pl.BlockSpec((tm, ROPE), lambda i, j, k: (i + rbo, 0))),
        out_shapes=[jax.ShapeDtypeStruct((B, N_HEADS_M, Lx, KD), BF16),
                    jax.ShapeDtypeStruct((B, N_HEADS_M, Lx, V_HEAD), BF16)],
        out_specs=[pl.BlockSpec((None, None, tm, KD), lambda i, j, k: (i // nt, j, i % nt, 0)),
                   pl.BlockSpec((None, None, tm, V_HEAD), lambda i, j, k: (i // nt, j, i % nt, 0))],
        n_row_blocks=B * nt, row_block_off=rbo, name="mla_kv_up")


def _outproj_body(a0_ref, a1_ref, a2_ref, b_ref, res_ref, gate_ref, o_ref, acc_ref, *, bounds):
    k = pl.program_id(2)

    @pl.when(k == 0)
    def _():
        acc_ref[...] = jnp.zeros_like(acc_ref)

    for a_ref, (lo, hi) in zip((a0_ref, a1_ref, a2_ref), bounds):
        @pl.when((k >= lo) & (k < hi))
        def _():
            acc_ref[...] += jnp.dot(a_ref[...], b_ref[...], preferred_element_type=F32)

    @pl.when(k == pl.num_programs(2) - 1)
    def _():
        o_ref[...] = res_ref[...] + gate_ref[...] * acc_ref[...]


def out_proj(parts, w, res, gate, rows, rows_per_mod):
    N = w.shape[1]
    R = gate.shape[0]
    tm = _tile(math.gcd(rows, rows_per_mod), 1024, 2 * SUBLANES)
    tn = _tile(N, 1024, LANES)
    tk = _tile(math.gcd(*[p.shape[1] for p in parts]), 1024, LANES)
    bounds, lo = [], 0
    for p in parts:
        bounds.append((lo, lo + p.shape[1] // tk))
        lo = bounds[-1][1]

    def a_spec(b):
        return pl.BlockSpec((tm, tk), lambda i, j, k: (i, jnp.clip(k - b[0], 0, b[1] - b[0] - 1)))

    o_spec = pl.BlockSpec((tm, tn), lambda i, j, k: (i, j))
    return pl.pallas_call(
        functools.partial(_outproj_body, bounds=tuple(bounds)),
        grid=(rows // tm, N // tn, lo),
        in_specs=[a_spec(b) for b in bounds] + [
            pl.BlockSpec((tk, tn), lambda i, j, k: (k, j)), o_spec,
            pl.BlockSpec((None, 1, tn), lambda i, j, k: (jnp.minimum(i * tm // rows_per_mod, R - 1), 0, j))],
        out_specs=o_spec,
        out_shape=jax.ShapeDtypeStruct((rows, N), F32),
        scratch_shapes=[pltpu.VMEM((tm, tn), F32)],
        compiler_params=_params(("parallel", "parallel", "arbitrary")),
        name="out_proj")(*parts, w, res, gate.reshape(R, 1, N))


def _seq_row_block(b, i, t, L, Lc, nlat):
    nc = Lc // t
    return jnp.where(i < nc, (nlat + b * Lc) // t + i, (b * L) // t + (i - nc))


def _conv_body(u_ref, up_ref, un_ref, dw_ref, db_ref, g_ref, b_ref, o_ref, buf_ref, sh_ref, *, tl, rc, nc, i0):
    i = pl.program_id(1) + i0
    nt = pl.num_programs(1) + i0
    first_tile = (i == 0) | (i == nc)
    last_tile = (i == nc - 1) | (i == nt - 1)

    def glu(u):
        return u[:, :C_CONV] * jax.nn.sigmoid(u[:, C_CONV:])

    buf_ref[pl.ds(0, CONV_HALO), :] = jnp.where(first_tile, 0.0, glu(up_ref[...]))
    buf_ref[pl.ds(CONV_HALO, tl), :] = glu(u_ref[...])
    buf_ref[pl.ds(CONV_HALO + tl, CONV_HALO), :] = jnp.where(last_tile, 0.0, glu(un_ref[...]))
    nsh = tl + 2 * CONV_HALO - SUBLANES
    for s in range(1, SUBLANES):
        sh_ref[s - 1] = buf_ref[pl.ds(s, nsh), :]

    first = CONV_HALO - CONV_W // 2

    def chunk(c, carry):
        r0 = pl.multiple_of(c * rc, rc)
        acc = jnp.broadcast_to(db_ref[...], (rc, C_CONV))
        for k in range(CONV_W):
            q, s = divmod(first + k, SUBLANES)
            start = pl.multiple_of(r0 + q * SUBLANES, SUBLANES)
            win = buf_ref[pl.ds(start, rc), :] if s == 0 else sh_ref[s - 1, pl.ds(start, rc), :]
            acc = acc + dw_ref[pl.ds(k, 1), :] * win
        mu = jnp.mean(acc, axis=-1, keepdims=True)
        d = acc - mu
        var = jnp.mean(d * d, axis=-1, keepdims=True)
        y = d * lax.rsqrt(var + LN_EPS) * g_ref[...] + b_ref[...]
        o_ref[pl.ds(r0, rc), :] = _silu(y).astype(o_ref.dtype)
        return carry

    lax.fori_loop(0, tl // rc, chunk, 0)


def conformer_conv(u, dw, db, ln_g, ln_b, B, L, Lc, with_ctx):
    T = u.shape[0]
    nlat = B * L
    tl = _tile(math.gcd(L, Lc), 256, CONV_HALO)
    rc = _tile(tl, 32, SUBLANES)
    nc = Lc // tl
    i0 = 0 if with_ctx else nc
    hb = tl // CONV_HALO
    nhalo = T // CONV_HALO
    row = lambda b, i: _seq_row_block(b, i + i0, tl, L, Lc, nlat)
    body = functools.partial(_conv_body, tl=tl, rc=rc, nc=nc, i0=i0)
    vec = lambda a: a.reshape(1, C_CONV)
    vspec = pl.BlockSpec((1, C_CONV), lambda b, i: (0, 0))
    return pl.pallas_call(
        body,
        grid=(B, (L + Lc) // tl - i0),
        in_specs=[pl.BlockSpec((tl, 2 * C_CONV), lambda b, i: (row(b, i), 0)),
                  pl.BlockSpec((CONV_HALO, 2 * C_CONV), lambda b, i: (jnp.maximum(row(b, i) * hb - 1, 0), 0)),
                  pl.BlockSpec((CONV_HALO, 2 * C_CONV), lambda b, i: (jnp.minimum((row(b, i) + 1) * hb, nhalo - 1), 0)),
                  pl.BlockSpec((CONV_W, C_CONV), lambda b, i: (0, 0)),
                  vspec, vspec, vspec],
        out_specs=pl.BlockSpec((tl, C_CONV), lambda b, i: (row(b, i), 0)),
        out_shape=jax.ShapeDtypeStruct((T, C_CONV), BF16),
        scratch_shapes=[pltpu.VMEM((tl + 2 * CONV_HALO, C_CONV), F32),
                        pltpu.VMEM((SUBLANES - 1, tl + 2 * CONV_HALO - SUBLANES, C_CONV), F32)],
        compiler_params=_params(("parallel", "arbitrary")),
        name="conformer_conv",
    )(u, u, u, dw, vec(db), vec(ln_g), vec(ln_b))


def _softplus(z):
    return jnp.maximum(z, 0.0) + jnp.log(1.0 + jnp.exp(-jnp.abs(z)))


def _feat_body(pr_ref, prv_ref, nxt_ref, mup_ref, mun_ref, w0_ref, w2_ref, a0_ref, a2_ref, kkp_ref, ka_ref,
               rk_ref, g2_ref, bd_ref, p1f_ref, p1b_ref, p2f_ref, p2b_ref, p3_ref, bonus_ref, gate_ref, buf_ref,
               *, tm, nc):
    i = pl.program_id(1)
    nt = pl.num_programs(1)
    first_tile = (i == 0) | (i == nc)
    last_tile = (i == nc - 1) | (i == nt - 1)
    buf_ref[pl.ds(SUBLANES, tm), :] = pr_ref[...]
    buf_ref[pl.ds(0, SUBLANES), :] = jnp.where(first_tile, 0.0, prv_ref[...])
    buf_ref[pl.ds(SUBLANES + tm, SUBLANES), :] = jnp.where(last_tile, 0.0, nxt_ref[...])

    def shifted(lo, hi):
        p = buf_ref[pl.ds(SUBLANES, tm), lo:hi]
        pv = buf_ref[pl.ds(SUBLANES - 1, tm), lo:hi]
        nx = buf_ref[pl.ds(SUBLANES + 1, tm), lo:hi]
        return p + mup_ref[:, lo:hi] * (pv - p) + mun_ref[:, lo:hi] * (nx - p)

    def head_sum(x):
        return jnp.concatenate(
            [jnp.dot(x[:, c * LANES:(c + 1) * LANES], bd_ref[...], precision=HP, preferred_element_type=F32)
             for c in range(C_R // LANES)], axis=-1)

    r = shifted(RC_R, RC_K)
    k = shifted(RC_K, RC_V)
    v = shifted(RC_V, RC_W)
    wd = jnp.tanh(shifted(RC_W, RC_A))
    ad = shifted(RC_A, RC_G)
    rg = shifted(RC_G, RCOLS)
    wl = -_softplus(-(w0_ref[...] + jnp.dot(wd, w2_ref[...], precision=HP, preferred_element_type=F32))) - 0.5
    dec = jnp.exp(-jnp.exp(wl))
    a = jax.nn.sigmoid(a0_ref[...] + jnp.dot(ad, a2_ref[...], precision=HP, preferred_element_type=F32))
    kkv = k * kkp_ref[...]
    kk = kkv / jnp.maximum(jnp.sqrt(head_sum(kkv * kkv)), 1e-12)
    gate_ref[...] = jnp.dot(jax.nn.sigmoid(rg), g2_ref[...], precision=HP, preferred_element_type=F32)

    lane = lax.broadcasted_iota(jnp.int32, (tm, LANES), 1)
    low = lane < HEAD_R

    def pack(o_ref, x, y):
        heads = []
        for c in range(C_R // LANES):
            xa = x[:, c * LANES:(c + 1) * LANES]
            ya = y[:, c * LANES:(c + 1) * LANES]
            heads.append(jnp.where(low, xa, pltpu.roll(ya, HEAD_R, axis=1)))
            heads.append(jnp.where(low, pltpu.roll(xa, HEAD_R, axis=1), ya))
        o_ref[...] = pltpu.einshape("htl->thl", jnp.stack(heads, axis=0))

    bonus = jnp.zeros((tm, C_R), F32)
    for d, (o1, o2) in enumerate(((p1f_ref, p2f_ref), (p1b_ref, p2b_ref))):
        a_d = a[:, d * C_R:(d + 1) * C_R]
        k_d = k * (1.0 + (a_d - 1.0) * ka_ref[...])
        pack(o1, dec[:, d * C_R:(d + 1) * C_R], kk * a_d)
        pack(o2, k_d, kk)
        bonus = bonus + head_sum(r * k_d * rk_ref[...]) * v
    pack(p3_ref, v, r)
    bonus_ref[...] = bonus


def rwkv_features(pr, rp, B, L, Lc):
    T = pr.shape[0]
    nlat = B * L
    Lt = L + Lc
    tm = _tile(math.gcd(L, Lc), 256, SUBLANES)
    nc = Lc // tm
    nrow8 = T // SUBLANES
    hb = tm // SUBLANES
    row = lambda b, i: _seq_row_block(b, i, tm, L, Lc, nlat)
    full = lambda a: pl.BlockSpec(a.shape, lambda b, i: (0,) * a.ndim)
    consts = (rp["mu_prev"], rp["mu_next"], rp["w0"], rp["w2"], rp["a0"], rp["a2"], rp["kk"], rp["ka"], rp["rk"],
              rp["g2"], rp["bd"])
    pk_shape = jax.ShapeDtypeStruct((Lt, B * N_HEADS_R, 2 * HEAD_R), F32)
    pk_spec = pl.BlockSpec((tm, N_HEADS_R, 2 * HEAD_R), lambda b, i: (i, b, 0))
    tok_shape = jax.ShapeDtypeStruct((T, C_R), F32)
    tok_spec = pl.BlockSpec((tm, C_R), lambda b, i: (row(b, i), 0))
    return pl.pallas_call(
        functools.partial(_feat_body, tm=tm, nc=nc),
        grid=(B, Lt // tm),
        in_specs=[pl.BlockSpec((tm, RCOLS), lambda b, i: (row(b, i), 0)),
                  pl.BlockSpec((SUBLANES, RCOLS), lambda b, i: (jnp.maximum(row(b, i) * hb - 1, 0), 0)),
                  pl.BlockSpec((SUBLANES, RCOLS), lambda b, i: (jnp.minimum((row(b, i) + 1) * hb, nrow8 - 1), 0))]
                 + [full(a) for a in consts],
        out_specs=[pk_spec] * 5 + [tok_spec] * 2,
        out_shape=[pk_shape] * 5 + [tok_shape] * 2,
        scratch_shapes=[pltpu.VMEM((tm + 2 * SUBLANES, RCOLS), F32)],
        compiler_params=_params(("parallel", "arbitrary")),
        name="rwkv_features",
    )(pr, pr, pr, *consts)


def _scan_body(p1f_ref, p1b_ref, p2f_ref, p2b_ref, p3f_ref, p3b_ref, yf_ref, yb_ref, s_ref, op_ref, ybuf_ref,
               *, tc, n, nch):
    @pl.when(pl.program_id(0) == 0)
    def _():
        s_ref[...] = jnp.zeros_like(s_ref)

    def step(t, carry):
        tb = tc - 1 - t
        for idx, (f_ref, b_ref) in enumerate(((p1f_ref, p1b_ref), (p2f_ref, p2b_ref), (p3f_ref, p3b_ref))):
            op_ref[idx] = jnp.concatenate([f_ref[t], b_ref[tb]], axis=0).T
        w = op_ref[0, pl.ds(0, n), :]
        kka = op_ref[0, pl.ds(n, n), :]
        k = op_ref[1, pl.ds(0, n), :]
        kk = op_ref[1, pl.ds(n, n), :]
        r = op_ref[2, pl.ds(n, n), :]

        def rows(ib, c2):
            ys = []
            for ii in range(SUBLANES):
                i = ib * SUBLANES + ii
                s = s_ref[i]
                sa = jnp.sum(s * kk, axis=0, keepdims=True)
                vi = op_ref[2, pl.ds(i, 1), :]
                s_new = s * w - sa * kka + vi * k
                s_ref[i] = s_new
                ys.append(jnp.sum(s_new * r, axis=0, keepdims=True))
            ybuf_ref[pl.ds(pl.multiple_of(ib * SUBLANES, SUBLANES), SUBLANES), :] = jnp.concatenate(ys, axis=0)
            return c2

        lax.fori_loop(0, n // SUBLANES, rows, 0)
        y = ybuf_ref[...]
        yt = jnp.concatenate([y, y], axis=0).T
        yf_ref[t] = yt[:nch, :n]
        yb_ref[tb] = yt[nch:, :n]
        return carry

    lax.fori_loop(0, tc, step, 0)


def wkv_scan(p1f, p1b, p2f, p2b, p3, Lc):
    Lt, nch, _ = p1f.shape
    n = HEAD_R
    assert 2 * nch == LANES and 2 * n == LANES
    tc = _tile(math.gcd(Lc, Lt - Lc), 32, 1)
    ncb, nb = Lc // tc, Lt // tc
    fwd = lambda g: (g, 0, 0)
    bwd = lambda g: (jnp.where(g < ncb, ncb - 1 - g, nb - 1 - (g - ncb)), 0, 0)
    ispec = lambda m: pl.BlockSpec((tc, nch, 2 * n), m)
    ospec = lambda m: pl.BlockSpec((tc, nch, n), m)
    y_shape = jax.ShapeDtypeStruct((Lt, nch, n), F32)
    return pl.pallas_call(
        functools.partial(_scan_body, tc=tc, n=n, nch=nch),
        grid=(nb,),
        in_specs=[ispec(fwd), ispec(bwd), ispec(fwd), ispec(bwd), ispec(fwd), ispec(bwd)],
        out_specs=[ospec(fwd), ospec(bwd)],
        out_shape=[y_shape, y_shape],
        scratch_shapes=[pltpu.VMEM((n, n, LANES), F32), pltpu.VMEM((3, LANES, LANES), F32),
                        pltpu.VMEM((n, LANES), F32)],
        compiler_params=_params(("arbitrary",)),
        name="wkv_scan",
    )(p1f, p1b, p2f, p2b, p3, p3)


def _post_body(yf_ref, yb_ref, bonus_ref, gate_ref, g_ref, b_ref, o_ref):
    yh = pltpu.einshape("thl->htl", yf_ref[...] + yb_ref[...])
    cols = []
    for h in range(N_HEADS_R):
        y = yh[h]
        mu = jnp.mean(y, axis=-1, keepdims=True)
        d = y - mu
        var = jnp.mean(d * d, axis=-1, keepdims=True)
        cols.append(d * lax.rsqrt(var + GN_EPS))
    yn = jnp.concatenate(cols, axis=-1) * g_ref[...] + b_ref[...]
    o_ref[...] = ((yn + bonus_ref[...]) * gate_ref[...]).astype(o_ref.dtype)


def rwkv_output(yf, yb, bonus, gate, ln_g, ln_b, B, L, Lc, with_ctx):
    T = bonus.shape[0]
    nlat = B * L
    tm = _tile(math.gcd(L, Lc), 256, 2 * SUBLANES)
    i0 = 0 if with_ctx else Lc // tm
    row = lambda b, i: _seq_row_block(b, i + i0, tm, L, Lc, nlat)
    y_spec = pl.BlockSpec((tm, N_HEADS_R, HEAD_R), lambda b, i: (i + i0, b, 0))
    tok_spec = pl.BlockSpec((tm, C_R), lambda b, i: (row(b, i), 0))
    vspec = pl.BlockSpec((1, C_R), lambda b, i: (0, 0))
    return pl.pallas_call(
        _post_body,
        grid=(B, (L + Lc) // tm - i0),
        in_specs=[y_spec, y_spec, tok_spec, tok_spec, vspec, vspec],
        out_specs=tok_spec,
        out_shape=jax.ShapeDtypeStruct((T, C_R), BF16),
        compiler_params=_params(("parallel", "parallel")),
        name="rwkv_output",
    )(yf, yb, bonus, gate, ln_g.reshape(1, C_R), ln_b.reshape(1, C_R))


def _attn_body(*refs, n_kv):
    q = refs[0][...]
    kv = refs[1:1 + 2 * n_kv]
    o_ref = refs[-1]
    s = [lax.dot_general(q, kv[2 * j][...], (((1,), (1,)), ((), ())), preferred_element_type=F32) for j in range(n_kv)]
    m = functools.reduce(jnp.maximum, [jnp.max(x, axis=-1, keepdims=True) for x in s])
    p = [jnp.exp(x - m) for x in s]
    l = functools.reduce(jnp.add, [jnp.sum(x, axis=-1, keepdims=True) for x in p])
    o = functools.reduce(jnp.add, [jnp.dot(p[j].astype(BF16), kv[2 * j + 1][...], preferred_element_type=F32)
                                   for j in range(n_kv)])
    o_ref[...] = (o / l).astype(o_ref.dtype)


def mla_attention(q, kvs, rows_total, row_off, prev=None):
    B, H, Lq, dk = q.shape
    dv = kvs[0][1].shape[3]
    tq = _tile(Lq, 256, 2 * SUBLANES)
    nq = Lq // tq
    rbo = row_off // tq
    in_specs = [pl.BlockSpec((None, None, tq, dk), lambda b, h, i: (b, h, i, 0))]
    args = [q]
    for k, v in kvs:
        in_specs += [pl.BlockSpec((None, None, k.shape[2], dk), lambda b, h, i: (b, h, 0, 0)),
                     pl.BlockSpec((None, None, v.shape[2], dv), lambda b, h, i: (b, h, 0, 0))]
        args += [k, v]
    body = functools.partial(_attn_body, n_kv=len(kvs))
    aliases = {}
    if prev is not None:
        in_specs.append(pl.BlockSpec(memory_space=pl.ANY))
        args.append(prev)
        aliases = {len(args) - 1: 0}
        body = lambda *refs: _attn_body(*refs[:-2], refs[-1], n_kv=len(kvs))
    return pl.pallas_call(
        body,
        grid=(B, H, nq),
        in_specs=in_specs,
        out_specs=pl.BlockSpec((tq, dv), lambda b, h, i: (rbo + b * nq + i, h)),
        out_shape=jax.ShapeDtypeStruct((rows_total, H * dv), BF16),
        input_output_aliases=aliases,
        compiler_params=_params(("parallel", "parallel", "arbitrary")),
        name="mla_attention",
    )(*args)


def _moe_body(be_ref, nu_ref, x_ref, w1_ref, w3_ref, w2_ref, o_ref):
    blk = pl.program_id(0)
    f = pl.program_id(1)

    @pl.when(f == 0)
    def _():
        o_ref[...] = jnp.zeros_like(o_ref)

    @pl.when(blk < nu_ref[0])
    def _():
        x = x_ref[...]
        h1 = jnp.dot(x, w1_ref[...], preferred_element_type=F32)
        h3 = jnp.dot(x, w3_ref[...], preferred_element_type=F32)
        hb = (_silu(h1) * h3).astype(BF16)
        o_ref[...] += jnp.dot(hb, w2_ref[...], preferred_element_type=F32)


def moe_experts(xs, block_e, n_used, w1, w3, w2):
    P, D = xs.shape
    Fd = w1.shape[2]
    tf = _tile(Fd, 256, LANES)
    nf = Fd // tf
    nb = P // MOE_ROWS

    def fidx(i, f, nu):
        return jnp.where(i < nu[0], f, nf - 1)

    grid_spec = pltpu.PrefetchScalarGridSpec(
        num_scalar_prefetch=2,
        grid=(nb, nf),
        in_specs=[pl.BlockSpec((MOE_ROWS, D), lambda i, f, be, nu: (i, 0)),
                  pl.BlockSpec((None, D, tf), lambda i, f, be, nu: (be[i], 0, fidx(i, f, nu))),
                  pl.BlockSpec((None, D, tf), lambda i, f, be, nu: (be[i], 0, fidx(i, f, nu))),
                  pl.BlockSpec((None, tf, D), lambda i, f, be, nu: (be[i], fidx(i, f, nu), 0))],
        out_specs=pl.BlockSpec((MOE_ROWS, D), lambda i, f, be, nu: (i, 0)),
    )
    return pl.pallas_call(
        _moe_body,
        grid_spec=grid_spec,
        out_shape=jax.ShapeDtypeStruct((P, D), F32),
        compiler_params=_params(("arbitrary", "arbitrary")),
        name="moe_experts",
    )(block_e, n_used, xs, w1, w3, w2)


def _route(tok, w_router, router_bias):
    logits = jnp.dot(tok, w_router, precision=HP)
    scores = jax.nn.sigmoid(logits.astype(F32))
    biased = (scores + router_bias.astype(F32)).reshape(-1, N_GROUPS, EXPERTS_PER_GROUP)
    group_score = jnp.sum(lax.top_k(biased, 2)[0], -1)
    chosen = jnp.argmax(group_score, -1)
    in_group = jnp.arange(N_GROUPS)[None, :] == chosen[:, None]
    masked = jnp.where(in_group[:, :, None], biased, -jnp.inf).reshape(-1, N_EXPERTS)
    _, idx = lax.top_k(masked, TOP_K)
    w = jnp.take_along_axis(scores, idx, -1)
    return idx, w / jnp.sum(w, -1, keepdims=True)


def moe_ffn(tok, w_router, router_bias, w1, w3, w2):
    T, D = tok.shape
    idx, wts = _route(tok, w_router, router_bias)
    M = T * TOP_K
    flat_e = idx.reshape(-1).astype(jnp.int32)
    onehot = (flat_e[:, None] == jnp.arange(N_EXPERTS, dtype=jnp.int32)[None, :]).astype(jnp.int32)
    csum = jnp.cumsum(onehot, axis=0)
    rank = jnp.take_along_axis(csum, flat_e[:, None], axis=1)[:, 0] - 1
    counts = csum[-1]
    pcounts = (counts + MOE_ROWS - 1) // MOE_ROWS * MOE_ROWS
    pends = jnp.cumsum(pcounts)
    pstarts = pends - pcounts
    dest = pstarts[flat_e] + rank
    n_blocks = -(-M // MOE_ROWS) + N_EXPERTS
    P = n_blocks * MOE_ROWS
    flat_t = jnp.repeat(jnp.arange(T, dtype=jnp.int32), TOP_K)
    row_tok = jnp.zeros((P,), jnp.int32).at[dest].set(flat_t)
    n_used = (pends[-1] // MOE_ROWS).astype(jnp.int32)
    blk_start = jnp.arange(n_blocks, dtype=jnp.int32) * MOE_ROWS
    block_e = jnp.clip(jnp.searchsorted(pends, blk_start, side="right"), 0, N_EXPERTS - 1).astype(jnp.int32)
    last_e = block_e[jnp.maximum(n_used - 1, 0)]
    block_e = jnp.where(jnp.arange(n_blocks) < n_used, block_e, last_e)
    xs = tok.astype(BF16)[row_tok]
    yb = moe_experts(xs, block_e, n_used.reshape(1), w1, w3, w2)
    pair = yb[dest].reshape(T, TOP_K, D)
    return jnp.sum(pair * wts[:, :, None], axis=1)


def _rope_tables(rows):
    half = ROPE // 2
    inv_freq = ROPE_BASE ** (-jnp.arange(0, half, 2, dtype=F32) / half)
    r = jnp.repeat(jnp.arange(rows, dtype=F32), GRID_W)
    cl = jnp.tile(jnp.arange(GRID_W, dtype=F32), rows)
    ar = r[:, None] * inv_freq
    ac = cl[:, None] * inv_freq
    ang = jnp.concatenate([ar, ar, ac, ac], -1)
    return jnp.cos(ang), jnp.sin(ang)


def _rot_cols(w):
    q = ROPE // 4
    return jnp.concatenate([-w[..., q:2 * q], w[..., :q], -w[..., 3 * q:], w[..., 2 * q:3 * q]], -1)


def _rot_perm(g):
    q = ROPE // 4
    return jnp.concatenate([g[q:2 * q], g[:q], g[3 * q:], g[2 * q:3 * q]], -1)


def _rope_tab(g, cos, sin):
    if cos is None:
        return jnp.concatenate([g, jnp.zeros_like(g)])[None, :]
    return jnp.concatenate([g[None, :] * cos, _rot_perm(g)[None, :] * sin], -1)


def _blockdiag2(w):
    z = jnp.zeros_like(w[0])
    return jnp.concatenate([jnp.concatenate([w[0], z], 1), jnp.concatenate([z, w[1]], 1)], 0)


def kernel(x, c, ctx, c_ctx, w_mod, b_mod, norm1_g, norm2_g, w_in, w_out, conv_dw, conv_b, conv_ln_g, conv_ln_b, r_mu, r_w0, r_w2, r_a0, r_a2, r_g2, r_kk, r_ka, r_rk, r_ln_g, r_ln_b, m_cq_g, m_w_uq, m_ckv_g, m_w_ukv, m_qn_g, m_qr_g, m_kn_g, m_kr_g, w_router, router_bias, moe_w1, moe_w3, moe_w2):
    B, L, D = x.shape
    Lc = ctx.shape[1]
    depth = w_mod.shape[0]
    nlat, nctx = B * L, B * Lc
    T = nlat + nctx
    cos, sin = _rope_tables(L // GRID_W)
    tm_in = _tile(math.gcd(L, nctx), 512, 2 * SUBLANES)
    bd = jnp.kron(jnp.eye(LANES // HEAD_R, dtype=F32), jnp.ones((HEAD_R, HEAD_R), F32))

    H = jnp.concatenate([x.reshape(nlat, D), ctx.reshape(nctx, D)], 0)
    crow = jnp.concatenate([c, c_ctx[None, :], jnp.zeros((SUBLANES - (B + 1) % SUBLANES, D), F32)], 0)

    for l in range(depth):
        last = l == depth - 1
        mod = adaln_mod(crow, w_mod[l], b_mod[l])[:B + 1]
        sh1, sc1, ga1, sh2, sc2, ga2 = jnp.split(mod, 6, axis=-1)
        n = norm_mod(H, norm1_g[l], sc1, sh1, L)

        wi = w_in[l]
        cols = lambda lo, hi: wi[:, lo:hi]
        w_conv = cast_bf16(cols(OFF_CONV, OFF_MQ))
        w_mq = cast_bf16(cols(OFF_MQ, OFF_RR))
        w_rw = cast_bf16(jnp.concatenate(
            [cols(OFF_RR, OFF_RG), cols(OFF_RK, OFF_RV), cols(OFF_RV, OFF_RW), cols(OFF_RW, OFF_RA),
             cols(OFF_RA, OFF_MKV), cols(OFF_RG, OFF_RK), jnp.zeros((D, G_PAD - G_LORA), F32)], 1))
        w_kv = cast_bf16(jnp.concatenate(
            [cols(OFF_MKV, OFF_MKR), cols(OFF_MKR, IN_COLS), _rot_cols(cols(OFF_MKR, IN_COLS))], 1))
        u = matmul_plain(n, w_conv, name="in_conv")
        pr = matmul_plain(n, w_rw, name="in_rwkv")
        cq = matmul(n, w_mq, epi=_epi_mq, tm=tm_in, tn=Q_LORA,
                    extra=(m_cq_g[l].reshape(1, Q_LORA),),
                    extra_specs=(pl.BlockSpec((1, Q_LORA), lambda i, j, k: (0, 0)),),
                    out_shapes=[jax.ShapeDtypeStruct((T, Q_LORA), BF16)],
                    out_specs=[pl.BlockSpec((tm_in, Q_LORA), lambda i, j, k: (i, 0))], name="in_mq")[0]
        ktab = jnp.concatenate([_rope_tab(m_kr_g[l], cos, sin),
                                jnp.broadcast_to(_rope_tab(m_kr_g[l], None, None), (tm_in, 2 * ROPE))], 0)
        nlb = L // tm_in
        ckv, kr = matmul(n, w_kv, epi=_epi_mkv, tm=tm_in, tn=KV_LORA + 2 * ROPE,
                         extra=(m_ckv_g[l].reshape(1, KV_LORA), ktab),
                         extra_specs=(pl.BlockSpec((1, KV_LORA), lambda i, j, k: (0, 0)),
                                      pl.BlockSpec((tm_in, 2 * ROPE),
                                                   lambda i, j, k: (jnp.where(i < B * nlb, i % nlb, nlb), 0))),
                         out_shapes=[jax.ShapeDtypeStruct((T, KV_LORA), BF16), jax.ShapeDtypeStruct((T, ROPE), F32)],
                         out_specs=[pl.BlockSpec((tm_in, KV_LORA), lambda i, j, k: (i, 0)),
                                    pl.BlockSpec((tm_in, ROPE), lambda i, j, k: (i, 0))], name="in_mkv")

        conv = conformer_conv(u, conv_dw[l], conv_b[l], conv_ln_g[l], conv_ln_b[l], B, L, Lc, not last)

        mu = r_mu[l]
        mcols = lambda lo, hi: mu[:, lo - OFF_RR:hi - OFF_RR]
        mu_r = jnp.concatenate([mcols(OFF_RR, OFF_RG), mcols(OFF_RK, OFF_RV), mcols(OFF_RV, OFF_RW),
                                mcols(OFF_RW, OFF_RA), mcols(OFF_RA, OFF_MKV), mcols(OFF_RG, OFF_RK),
                                jnp.zeros((2, G_PAD - G_LORA), F32)], 1)
        rp = {"mu_prev": mu_r[0:1], "mu_next": mu_r[1:2],
              "w0": r_w0[l].reshape(1, 2 * C_R), "w2": _blockdiag2(r_w2[l]),
              "a0": r_a0[l].reshape(1, 2 * C_R), "a2": _blockdiag2(r_a2[l]),
              "kk": r_kk[l].reshape(1, C_R), "ka": r_ka[l].reshape(1, C_R), "rk": r_rk[l].reshape(1, C_R),
              "g2": jnp.pad(r_g2[l], ((0, G_PAD - G_LORA), (0, 0))), "bd": bd}
        p1f, p1b, p2f, p2b, p3, bonus, gate = rwkv_features(pr, rp, B, L, Lc)
        yf, yb = wkv_scan(p1f, p1b, p2f, p2b, p3, Lc)
        rw = rwkv_output(yf, yb, bonus, gate, r_ln_g[l], r_ln_b[l], B, L, Lc, not last)

        wq = m_w_uq[l].reshape(Q_LORA, N_HEADS_M, NOPE + ROPE)
        wq = cast_bf16(jnp.concatenate([wq, _rot_cols(wq[..., NOPE:])], -1).reshape(Q_LORA, N_HEADS_M * QH))
        wkv = cast_bf16(m_w_ukv[l])
        k_c, v_c = mla_kv_up(ckv, wkv, m_kn_g[l], kr, B, Lc, nlat)
        k_l, v_l = mla_kv_up(ckv, wkv, m_kn_g[l], kr, B, L, 0)
        q_l = mla_q_up(cq, wq, m_qn_g[l], _rope_tab(m_qr_g[l], cos, sin), B, L, 0)
        rows_out = nlat if last else T
        mla = mla_attention(q_l, [(k_c, v_c), (k_l, v_l)], rows_out, 0)
        if not last:
            tq_c = _tile(Lc, 512, 2 * SUBLANES)
            q_c = mla_q_up(cq, wq, m_qn_g[l], jnp.broadcast_to(_rope_tab(m_qr_g[l], None, None), (tq_c, 2 * ROPE)),
                           B, Lc, nlat)
            mla = mla_attention(q_c, [(k_c, v_c)], rows_out, nlat, prev=mla)

        Hn = out_proj((conv, rw, mla), cast_bf16(w_out[l]), H, ga1, rows_out, L)
        n2 = norm_mod(Hn, norm2_g[l], sc2, sh2, L, out_dtype=F32)
        y = moe_ffn(n2, w_router, router_bias, cast_bf16(moe_w1[l]), cast_bf16(moe_w3[l]), cast_bf16(moe_w2[l]))
        grow = jnp.repeat(ga2[:B], L, axis=0)
        if not last:
            grow = jnp.concatenate([grow, jnp.repeat(ga2[B:], nctx, axis=0)], 0)
        H = Hn + grow * y
    return H[:nlat].reshape(B, L, D)
```

```python
import functools
import math

import jax
import jax.numpy as jnp
from jax import lax
from jax.experimental import pallas as pl
from jax.experimental.pallas import tpu as pltpu

GRID_W = 64
C_CONV = 1024
CONV_W = 31
N_HEADS_R = 16
HEAD_R = 64
C_R = N_HEADS_R * HEAD_R
W_LORA = 64
A_LORA = 64
G_LORA = 160
GN_EPS = 64e-5
N_HEADS_M = 16
NOPE = 128
ROPE = 64
V_HEAD = 128
Q_LORA = 1536
KV_LORA = 512
C_M = N_HEADS_M * V_HEAD
ROPE_BASE = 10000.0
ATTN_SCALE = (NOPE + ROPE) ** -0.5
Q_SCALE = ATTN_SCALE * math.log2(math.e)
RMS_EPS = 1e-6
LN_EPS = 1e-5

OFF_CONV = 0
OFF_MQ = OFF_CONV + 2 * C_CONV
OFF_RR = OFF_MQ + Q_LORA
OFF_RG = OFF_RR + C_R
OFF_RK = OFF_RG + G_LORA
OFF_RV = OFF_RK + C_R
OFF_RW = OFF_RV + C_R
OFF_RA = OFF_RW + 2 * W_LORA
OFF_MKV = OFF_RA + 2 * A_LORA
OFF_MKR = OFF_MKV + KV_LORA
IN_COLS = OFF_MKR + ROPE

N_EXPERTS = 16
N_GROUPS = 4
EXPERTS_PER_GROUP = N_EXPERTS // N_GROUPS
TOP_K = 2

LANES = 128
SUBLANES = 8
VMEM_LIMIT = 56 * 1024 * 1024
CONV_HALO = 16
MOE_ROWS = 512
CAST_BLOCK_ELEMS = 2 * 1024 * 1024
MLA_HEAD_GROUP = 4

RC_R, RC_K, RC_V, RC_W, RC_A, RC_G = 0, C_R, 2 * C_R, 3 * C_R, 3 * C_R + 2 * W_LORA, 3 * C_R + 2 * W_LORA + 2 * A_LORA
G_PAD = 2 * LANES
RCOLS = RC_G + G_PAD
QH = NOPE + 2 * ROPE
KD = NOPE + ROPE

BF16 = jnp.bfloat16
F32 = jnp.float32
HP = lax.Precision.HIGHEST


def _params(sem):
    return pltpu.CompilerParams(dimension_semantics=sem, vmem_limit_bytes=VMEM_LIMIT)


def _tile(n, target, mult):
    best = None
    for d in range(mult, min(n, target) + 1, mult):
        if n % d == 0:
            best = d
    return n if best is None else best


def _silu(x):
    return x * jax.nn.sigmoid(x)


def _rms(x, g):
    return x * lax.rsqrt(jnp.mean(x * x, axis=-1, keepdims=True) + RMS_EPS) * g


def _cast_body(x_ref, o_ref):
    o_ref[...] = x_ref[...].astype(o_ref.dtype)


def cast_bf16(w, layer=None, col_off=0, cols=None):
    w3 = w[None] if layer is None else w
    lay = 0 if layer is None else layer
    lead = w3.shape[1:-1]
    w3 = w3.reshape(w3.shape[0], -1, w3.shape[-1])
    R = w3.shape[1]
    C = w3.shape[2] - col_off if cols is None else cols
    tc = _tile(math.gcd(C, col_off) if col_off else C, 4096, LANES)
    tr = _tile(R, max(CAST_BLOCK_ELEMS // tc, 2 * SUBLANES), 2 * SUBLANES)
    cb = col_off // tc
    out = pl.pallas_call(
        _cast_body, grid=(R // tr, C // tc),
        in_specs=[pl.BlockSpec((None, tr, tc), lambda i, j: (lay, i, j + cb))],
        out_specs=pl.BlockSpec((tr, tc), lambda i, j: (i, j)),
        out_shape=jax.ShapeDtypeStruct((R, C), BF16),
        compiler_params=_params(("parallel", "parallel")), name="cast_bf16")(w3)
    return out.reshape(lead + (C,))


def _mod_body(c_ref, w_ref, b_ref, o_ref):
    @pl.when(pl.program_id(0) == 0)
    def _():
        o_ref[...] = jnp.broadcast_to(b_ref[...], o_ref.shape)

    a = _silu(c_ref[...]).astype(BF16)
    o_ref[...] += jnp.dot(a, w_ref[...].astype(BF16), preferred_element_type=F32)


def adaln_mod(rows, w, b, layer):
    R, D = rows.shape
    N = w.shape[2]
    tk = LANES
    return pl.pallas_call(
        _mod_body,
        grid=(D // tk,),
        in_specs=[pl.BlockSpec((R, tk), lambda k: (0, k)),
                  pl.BlockSpec((None, tk, N), lambda k: (layer, k, 0)),
                  pl.BlockSpec((None, 1, N), lambda k: (layer, 0, 0))],
        out_specs=pl.BlockSpec((R, N), lambda k: (0, 0)),
        out_shape=jax.ShapeDtypeStruct((R, N), F32),
        compiler_params=_params(("arbitrary",)),
        name="adaln_mod",
    )(rows, w, b.reshape(b.shape[0], 1, N))


def _norm_mod_body(x_ref, g_ref, sc_ref, sh_ref, o_ref):
    o_ref[...] = (_rms(x_ref[...], g_ref[...]) * (1.0 + sc_ref[...]) + sh_ref[...]).astype(o_ref.dtype)


def norm_mod(h, g, sc, sh, rows_per_mod, out_dtype=BF16):
    T, D = h.shape
    R = sc.shape[0]
    tm = _tile(math.gcd(T, rows_per_mod), 256, 2 * SUBLANES)
    mod_row = lambda i: (jnp.minimum(i * tm // rows_per_mod, R - 1), 0, 0)
    return pl.pallas_call(
        _norm_mod_body,
        grid=(T // tm,),
        in_specs=[pl.BlockSpec((tm, D), lambda i: (i, 0)),
                  pl.BlockSpec((1, D), lambda i: (0, 0)),
                  pl.BlockSpec((None, 1, D), mod_row),
                  pl.BlockSpec((None, 1, D), mod_row)],
        out_specs=pl.BlockSpec((tm, D), lambda i: (i, 0)),
        out_shape=jax.ShapeDtypeStruct((T, D), out_dtype),
        compiler_params=_params(("parallel",)),
        name="norm_mod",
    )(h, g.reshape(1, D), sc.reshape(R, 1, D), sh.reshape(R, 1, D))


def _mm_body(*refs, n_extra, n_out, epi):
    a_ref, b_ref = refs[0], refs[1]
    extra = refs[2:2 + n_extra]
    outs = refs[2 + n_extra:2 + n_extra + n_out]
    acc_ref = refs[-1]
    k = pl.program_id(2)

    @pl.when(k == 0)
    def _():
        acc_ref[...] = jnp.zeros_like(acc_ref)

    acc_ref[...] += jnp.dot(a_ref[...], b_ref[...], preferred_element_type=F32)

    @pl.when(k == pl.num_programs(2) - 1)
    def _():
        epi(acc_ref[...], extra, outs)


def matmul(a, b, *, epi, out_shapes, out_specs, tm, tn, tk=None, extra=(), extra_specs=(),
           n_row_blocks=None, row_block_off=0, name="matmul"):
    M, K = a.shape
    N = b.shape[1]
    tk = _tile(K, 1024, LANES) if tk is None else tk
    nrb = M // tm if n_row_blocks is None else n_row_blocks
    grid = (nrb, N // tn, K // tk)
    return pl.pallas_call(
        functools.partial(_mm_body, n_extra=len(extra), n_out=len(out_shapes), epi=epi),
        grid=grid,
        in_specs=[pl.BlockSpec((tm, tk), lambda i, j, k: (i + row_block_off, k)),
                  pl.BlockSpec((tk, tn), lambda i, j, k: (k, j))] + list(extra_specs),
        out_specs=list(out_specs), out_shape=list(out_shapes),
        scratch_shapes=[pltpu.VMEM((tm, tn), F32)],
        compiler_params=_params(("parallel", "parallel", "arbitrary")), name=name)(a, b, *extra)


def _epi_plain(acc, extra, outs):
    outs[0][...] = acc.astype(outs[0].dtype)


def matmul_plain(a, b, out_dtype=F32, name="matmul"):
    M, N = a.shape[0], b.shape[1]
    tm = _tile(M, 1024, 2 * SUBLANES)
    tn = _tile(N, 1024, LANES)
    return matmul(a, b, epi=_epi_plain, out_shapes=[jax.ShapeDtypeStruct((M, N), out_dtype)],
                  out_specs=[pl.BlockSpec((tm, tn), lambda i, j, k: (i, j))], tm=tm, tn=tn, name=name)[0]


def _rope_mix(t, tab):
    lane = lax.broadcasted_iota(jnp.int32, t.shape, 1)
    ss = jnp.sum(jnp.where(lane < ROPE, t * t, 0.0), axis=-1, keepdims=True)
    m = t * tab
    return (m + pltpu.roll(m, ROPE, axis=1)) * lax.rsqrt(ss * (1.0 / ROPE) + RMS_EPS)


def _epi_mq(acc, extra, outs):
    outs[0][...] = _rms(acc, extra[0][...]).astype(outs[0].dtype)


def _epi_mkv(acc, extra, outs):
    g_ref, tab_ref = extra
    outs[0][...] = _rms(acc[:, :KV_LORA], g_ref[...]).astype(outs[0].dtype)
    outs[1][...] = _rope_mix(acc[:, KV_LORA:], tab_ref[...])[:, :ROPE]


def _epi_qup(acc, extra, outs):
    g_ref, tab_ref = extra
    o = outs[0]
    for h in range(o.shape[0]):
        a = acc[:, h * QH:(h + 1) * QH]
        o[h, :, :NOPE] = (_rms(a[:, :NOPE], g_ref[...]) * Q_SCALE).astype(o.dtype)
        o[h, :, NOPE:] = (_rope_mix(a[:, NOPE:], tab_ref[...])[:, :ROPE] * Q_SCALE).astype(o.dtype)


def _epi_kvup(acc, extra, outs):
    g_ref, kr_ref = extra
    ok, ov = outs
    kr = kr_ref[...].astype(ok.dtype)
    for h in range(ok.shape[0]):
        a = acc[:, h * (NOPE + V_HEAD):(h + 1) * (NOPE + V_HEAD)]
        ok[h, :, :NOPE] = _rms(a[:, :NOPE], g_ref[...]).astype(ok.dtype)
        ok[h, :, NOPE:] = kr
        ov[h] = a[:, NOPE:].astype(ov.dtype)


def _epi_res(acc, extra, outs):
    res_ref, gate_ref = extra
    outs[0][...] = res_ref[...] + gate_ref[...] * acc


def mla_q_up(cq, w, qn_g, tab, B, Lx, row_off):
    tm = _tile(Lx, 512, 2 * SUBLANES)
    nt = Lx // tm
    hg = MLA_HEAD_GROUP
    return matmul(
        cq, w, epi=_epi_qup, tm=tm, tn=hg * QH, tk=Q_LORA,
        extra=(qn_g.reshape(1, NOPE), tab),
        extra_specs=(pl.BlockSpec((1, NOPE), lambda i, j, k: (0, 0)),
                     pl.BlockSpec((tm, 2 * ROPE), lambda i, j, k: (i % nt, 0))),
        out_shapes=[jax.ShapeDtypeStruct((B, N_HEADS_M, Lx, KD), BF16)],
        out_specs=[pl.BlockSpec((None, hg, tm, KD), lambda i, j, k: (i // nt, j, i % nt, 0))],
        n_row_blocks=B * nt, row_block_off=row_off // tm, name="mla_q_up")[0]


def mla_kv_up(ckv, w, kn_g, kr, B, Lx, row_off):
    tm = _tile(Lx, 512, 2 * SUBLANES)
    nt = Lx // tm
    rbo = row_off // tm
    hg = MLA_HEAD_GROUP
    return matmul(
        ckv, w, epi=_epi_kvup, tm=tm, tn=hg * (NOPE + V_HEAD), tk=KV_LORA,
        extra=(kn_g.reshape(1, NOPE), kr),
        extra_specs=(pl.BlockSpec((1, NOPE), lambda i, j, k: (0, 0)),
                     pl.BlockSpec((tm, ROPE), lambda i, j, k: (i + rbo, 0))),
        out_shapes=[jax.ShapeDtypeStruct((B, N_HEADS_M, Lx, KD), BF16),
                    jax.ShapeDtypeStruct((B, N_HEADS_M, Lx, V_HEAD), BF16)],
        out_specs=[pl.BlockSpec((None, hg, tm, KD), lambda i, j, k: (i // nt, j, i % nt, 0)),
                   pl.BlockSpec((None, hg, tm, V_HEAD), lambda i, j, k: (i // nt, j, i % nt, 0))],
        n_row_blocks=B * nt, row_block_off=rbo, name="mla_kv_up")


def _outproj_body(a0_ref, a1_ref, a2_ref, b_ref, res_ref, gate_ref, o_ref, acc_ref, *, bounds):
    k = pl.program_id(2)

    @pl.when(k == 0)
    def _():
        acc_ref[...] = jnp.zeros_like(acc_ref)

    for a_ref, (lo, hi) in zip((a0_ref, a1_ref, a2_ref), bounds):
        @pl.when((k >= lo) & (k < hi))
        def _():
            acc_ref[...] += jnp.dot(a_ref[...], b_ref[...], preferred_element_type=F32)

    @pl.when(k == pl.num_programs(2) - 1)
    def _():
        o_ref[...] = res_ref[...] + gate_ref[...] * acc_ref[...]


def out_proj(parts, w, res, gate, rows, rows_per_mod):
    N = w.shape[1]
    R = gate.shape[0]
    tm = _tile(math.gcd(rows, rows_per_mod), 1024, 2 * SUBLANES)
    tn = _tile(N, 1024, LANES)
    tk = _tile(math.gcd(*[p.shape[1] for p in parts]), 1024, LANES)
    bounds, lo = [], 0
    for p in parts:
        bounds.append((lo, lo + p.shape[1] // tk))
        lo = bounds[-1][1]

    def a_spec(b):
        return pl.BlockSpec((tm, tk), lambda i, j, k: (i, jnp.clip(k - b[0], 0, b[1] - b[0] - 1)))

    o_spec = pl.BlockSpec((tm, tn), lambda i, j, k: (i, j))
    return pl.pallas_call(
        functools.partial(_outproj_body, bounds=tuple(bounds)),
        grid=(rows // tm, N // tn, lo),
        in_specs=[a_spec(b) for b in bounds] + [
            pl.BlockSpec((tk, tn), lambda i, j, k: (k, j)), o_spec,
            pl.BlockSpec((None, 1, tn), lambda i, j, k: (jnp.minimum(i * tm // rows_per_mod, R - 1), 0, j))],
        out_specs=o_spec,
        out_shape=jax.ShapeDtypeStruct((rows, N), F32),
        scratch_shapes=[pltpu.VMEM((tm, tn), F32)],
        compiler_params=_params(("parallel", "parallel", "arbitrary")),
        name="out_proj")(*parts, w, res, gate.reshape(R, 1, N))


def _seq_row_block(b, i, t, L, Lc, nlat):
    nc = Lc // t
    return jnp.where(i < nc, (nlat + b * Lc) // t + i, (b * L) // t + (i - nc))


def _conv_body(u_ref, up_ref, un_ref, dw_ref, db_ref, g_ref, b_ref, o_ref, buf_ref, sh_ref, *, tl, rc, nc, i0):
    i = pl.program_id(1) + i0
    nt = pl.num_programs(1) + i0
    first_tile = (i == 0) | (i == nc)
    last_tile = (i == nc - 1) | (i == nt - 1)

    def glu(u):
        return u[:, :C_CONV] * jax.nn.sigmoid(u[:, C_CONV:])

    buf_ref[pl.ds(0, CONV_HALO), :] = jnp.where(first_tile, 0.0, glu(up_ref[...]))
    buf_ref[pl.ds(CONV_HALO, tl), :] = glu(u_ref[...])
    buf_ref[pl.ds(CONV_HALO + tl, CONV_HALO), :] = jnp.where(last_tile, 0.0, glu(un_ref[...]))
    nsh = tl + 2 * CONV_HALO - SUBLANES
    for s in range(1, SUBLANES):
        sh_ref[s - 1] = buf_ref[pl.ds(s, nsh), :]

    first = CONV_HALO - CONV_W // 2

    def chunk(c, carry):
        r0 = pl.multiple_of(c * rc, rc)
        acc = jnp.broadcast_to(db_ref[...], (rc, C_CONV))
        for k in range(CONV_W):
            q, s = divmod(first + k, SUBLANES)
            start = pl.multiple_of(r0 + q * SUBLANES, SUBLANES)
            win = buf_ref[pl.ds(start, rc), :] if s == 0 else sh_ref[s - 1, pl.ds(start, rc), :]
            acc = acc + dw_ref[pl.ds(k, 1), :] * win
        mu = jnp.mean(acc, axis=-1, keepdims=True)
        d = acc - mu
        var = jnp.mean(d * d, axis=-1, keepdims=True)
        y = d * lax.rsqrt(var + LN_EPS) * g_ref[...] + b_ref[...]
        o_ref[pl.ds(r0, rc), :] = _silu(y).astype(o_ref.dtype)
        return carry

    lax.fori_loop(0, tl // rc, chunk, 0)


def conformer_conv(u, dw, db, ln_g, ln_b, B, L, Lc, with_ctx):
    T = u.shape[0]
    nlat = B * L
    tl = _tile(math.gcd(L, Lc), 256, CONV_HALO)
    rc = _tile(tl, 32, SUBLANES)
    nc = Lc // tl
    i0 = 0 if with_ctx else nc
    hb = tl // CONV_HALO
    nhalo = T // CONV_HALO
    row = lambda b, i: _seq_row_block(b, i + i0, tl, L, Lc, nlat)
    body = functools.partial(_conv_body, tl=tl, rc=rc, nc=nc, i0=i0)
    vec = lambda a: a.reshape(1, C_CONV)
    vspec = pl.BlockSpec((1, C_CONV), lambda b, i: (0, 0))
    return pl.pallas_call(
        body,
        grid=(B, (L + Lc) // tl - i0),
        in_specs=[pl.BlockSpec((tl, 2 * C_CONV), lambda b, i: (row(b, i), 0)),
                  pl.BlockSpec((CONV_HALO, 2 * C_CONV), lambda b, i: (jnp.maximum(row(b, i) * hb - 1, 0), 0)),
                  pl.BlockSpec((CONV_HALO, 2 * C_CONV), lambda b, i: (jnp.minimum((row(b, i) + 1) * hb, nhalo - 1), 0)),
                  pl.BlockSpec((CONV_W, C_CONV), lambda b, i: (0, 0)),
                  vspec, vspec, vspec],
        out_specs=pl.BlockSpec((tl, C_CONV), lambda b, i: (row(b, i), 0)),
        out_shape=jax.ShapeDtypeStruct((T, C_CONV), BF16),
        scratch_shapes=[pltpu.VMEM((tl + 2 * CONV_HALO, C_CONV), F32),
                        pltpu.VMEM((SUBLANES - 1, tl + 2 * CONV_HALO - SUBLANES, C_CONV), F32)],
        compiler_params=_params(("parallel", "arbitrary")),
        name="conformer_conv",
    )(u, u, u, dw, vec(db), vec(ln_g), vec(ln_b))


def _softplus(z):
    return jnp.maximum(z, 0.0) + jnp.log(1.0 + jnp.exp(-jnp.abs(z)))


def _feat_body(pr_ref, prv_ref, nxt_ref, mup_ref, mun_ref, w0_ref, w2_ref, a0_ref, a2_ref, kkp_ref, ka_ref,
               rk_ref, g2_ref, bd_ref, p1f_ref, p1b_ref, p2f_ref, p2b_ref, p3_ref, bonus_ref, gate_ref, buf_ref,
               *, tm, nc):
    i = pl.program_id(1)
    nt = pl.num_programs(1)
    first_tile = (i == 0) | (i == nc)
    last_tile = (i == nc - 1) | (i == nt - 1)
    buf_ref[pl.ds(SUBLANES, tm), :] = pr_ref[...]
    buf_ref[pl.ds(0, SUBLANES), :] = jnp.where(first_tile, 0.0, prv_ref[...])
    buf_ref[pl.ds(SUBLANES + tm, SUBLANES), :] = jnp.where(last_tile, 0.0, nxt_ref[...])

    def shifted(lo, hi):
        p = buf_ref[pl.ds(SUBLANES, tm), lo:hi]
        pv = buf_ref[pl.ds(SUBLANES - 1, tm), lo:hi]
        nx = buf_ref[pl.ds(SUBLANES + 1, tm), lo:hi]
        return p + mup_ref[:, lo:hi] * (pv - p) + mun_ref[:, lo:hi] * (nx - p)

    def head_sum(x):
        return jnp.concatenate(
            [jnp.dot(x[:, c * LANES:(c + 1) * LANES], bd_ref[...], precision=HP, preferred_element_type=F32)
             for c in range(C_R // LANES)], axis=-1)

    r = shifted(RC_R, RC_K)
    k = shifted(RC_K, RC_V)
    v = shifted(RC_V, RC_W)
    wd = jnp.tanh(shifted(RC_W, RC_A))
    ad = shifted(RC_A, RC_G)
    rg = shifted(RC_G, RCOLS)
    wl = -_softplus(-(w0_ref[...] + jnp.dot(wd, w2_ref[...], precision=HP, preferred_element_type=F32))) - 0.5
    dec = jnp.exp(-jnp.exp(wl))
    a = jax.nn.sigmoid(a0_ref[...] + jnp.dot(ad, a2_ref[...], precision=HP, preferred_element_type=F32))
    kkv = k * kkp_ref[...]
    kk = kkv / jnp.maximum(jnp.sqrt(head_sum(kkv * kkv)), 1e-12)
    gate_ref[...] = jnp.dot(jax.nn.sigmoid(rg), g2_ref[...], precision=HP, preferred_element_type=F32)

    lane = lax.broadcasted_iota(jnp.int32, (tm, LANES), 1)
    low = lane < HEAD_R

    def pack(o_ref, x, y):
        heads = []
        for c in range(C_R // LANES):
            xa = x[:, c * LANES:(c + 1) * LANES]
            ya = y[:, c * LANES:(c + 1) * LANES]
            heads.append(jnp.where(low, xa, pltpu.roll(ya, HEAD_R, axis=1)))
            heads.append(jnp.where(low, pltpu.roll(xa, HEAD_R, axis=1), ya))
        o_ref[...] = pltpu.einshape("htl->thl", jnp.stack(heads, axis=0))

    bonus = jnp.zeros((tm, C_R), F32)
    for d, (o1, o2) in enumerate(((p1f_ref, p2f_ref), (p1b_ref, p2b_ref))):
        a_d = a[:, d * C_R:(d + 1) * C_R]
        k_d = k * (1.0 + (a_d - 1.0) * ka_ref[...])
        pack(o1, dec[:, d * C_R:(d + 1) * C_R], kk * a_d)
        pack(o2, k_d, kk)
        bonus = bonus + head_sum(r * k_d * rk_ref[...]) * v
    pack(p3_ref, v, r)
    bonus_ref[...] = bonus


def rwkv_features(pr, rp, B, L, Lc):
    T = pr.shape[0]
    nlat = B * L
    Lt = L + Lc
    tm = _tile(math.gcd(L, Lc), 256, SUBLANES)
    nc = Lc // tm
    nrow8 = T // SUBLANES
    hb = tm // SUBLANES
    row = lambda b, i: _seq_row_block(b, i, tm, L, Lc, nlat)
    full = lambda a: pl.BlockSpec(a.shape, lambda b, i: (0,) * a.ndim)
    consts = (rp["mu_prev"], rp["mu_next"], rp["w0"], rp["w2"], rp["a0"], rp["a2"], rp["kk"], rp["ka"], rp["rk"],
              rp["g2"], rp["bd"])
    pk_shape = jax.ShapeDtypeStruct((Lt, B * N_HEADS_R, 2 * HEAD_R), F32)
    pk_spec = pl.BlockSpec((tm, N_HEADS_R, 2 * HEAD_R), lambda b, i: (i, b, 0))
    tok_shape = jax.ShapeDtypeStruct((T, C_R), F32)
    tok_spec = pl.BlockSpec((tm, C_R), lambda b, i: (row(b, i), 0))
    return pl.pallas_call(
        functools.partial(_feat_body, tm=tm, nc=nc),
        grid=(B, Lt // tm),
        in_specs=[pl.BlockSpec((tm, RCOLS), lambda b, i: (row(b, i), 0)),
                  pl.BlockSpec((SUBLANES, RCOLS), lambda b, i: (jnp.maximum(row(b, i) * hb - 1, 0), 0)),
                  pl.BlockSpec((SUBLANES, RCOLS), lambda b, i: (jnp.minimum((row(b, i) + 1) * hb, nrow8 - 1), 0))]
                 + [full(a) for a in consts],
        out_specs=[pk_spec] * 5 + [tok_spec] * 2,
        out_shape=[pk_shape] * 5 + [tok_shape] * 2,
        scratch_shapes=[pltpu.VMEM((tm + 2 * SUBLANES, RCOLS), F32)],
        compiler_params=_params(("parallel", "arbitrary")),
        name="rwkv_features",
    )(pr, pr, pr, *consts)


def _fold8(parts):
    sub = lax.broadcasted_iota(jnp.int32, parts[0].shape, 0)
    d = 1
    while len(parts) > 1:
        m = (sub // d) % 2 == 0
        parts = [jnp.where(m, a, b) + pltpu.roll(jnp.where(m, b, a), d, axis=0)
                 for a, b in zip(parts[0::2], parts[1::2])]
        d *= 2
    return parts[0]


def _scan_body(p1f_ref, p1b_ref, p2f_ref, p2b_ref, p3f_ref, p3b_ref, yf_ref, yb_ref, s_ref, op_ref, ybuf_ref,
               *, tc, n, nch):
    @pl.when(pl.program_id(0) == 0)
    def _():
        s_ref[...] = jnp.zeros_like(s_ref)

    def prep(t, carry):
        tb = tc - 1 - t
        for idx, (f_ref, b_ref) in enumerate(((p1f_ref, p1b_ref), (p2f_ref, p2b_ref), (p3f_ref, p3b_ref))):
            op_ref[t, idx] = jnp.concatenate([f_ref[t], b_ref[tb]], axis=0).T
        return carry

    lax.fori_loop(0, tc, prep, 0, unroll=2)
    nv = n // SUBLANES

    def colsum(x):
        return jnp.sum(x.reshape(nv, SUBLANES, x.shape[-1]), axis=0)

    def step(t, carry):
        w = op_ref[t, 0, pl.ds(0, n), :]
        kka = op_ref[t, 0, pl.ds(n, n), :]
        k = op_ref[t, 1, pl.ds(0, n), :]
        kk = op_ref[t, 1, pl.ds(n, n), :]
        r = op_ref[t, 2, pl.ds(n, n), :]

        def rows(ib, c2):
            base = pl.multiple_of(ib * SUBLANES, SUBLANES)
            sa8 = _fold8([colsum(s_ref[base + ii] * kk) for ii in range(SUBLANES)])
            v8 = op_ref[t, 2, pl.ds(base, SUBLANES), :]
            py = []
            for ii in range(SUBLANES):
                s_new = s_ref[base + ii] * w - sa8[ii:ii + 1] * kka + v8[ii:ii + 1] * k
                s_ref[base + ii] = s_new
                py.append(colsum(s_new * r))
            ybuf_ref[t, pl.ds(base, SUBLANES), :] = _fold8(py)
            return c2

        lax.fori_loop(0, nv, rows, 0)
        return carry

    lax.fori_loop(0, tc, step, 0)

    def fin(t, carry):
        y = ybuf_ref[t]
        yt = jnp.concatenate([y, y], axis=0).T
        yf_ref[t] = yt[:nch, :n]
        yb_ref[tc - 1 - t] = yt[nch:, :n]
        return carry

    lax.fori_loop(0, tc, fin, 0, unroll=2)


def wkv_scan(p1f, p1b, p2f, p2b, p3, Lc):
    Lt, nch, _ = p1f.shape
    n = HEAD_R
    assert 2 * nch == LANES and 2 * n == LANES
    tc = _tile(math.gcd(Lc, Lt - Lc), 32, 1)
    ncb, nb = Lc // tc, Lt // tc
    fwd = lambda g: (g, 0, 0)
    bwd = lambda g: (jnp.where(g < ncb, ncb - 1 - g, nb - 1 - (g - ncb)), 0, 0)
    ispec = lambda m: pl.BlockSpec((tc, nch, 2 * n), m)
    ospec = lambda m: pl.BlockSpec((tc, nch, n), m)
    y_shape = jax.ShapeDtypeStruct((Lt, nch, n), F32)
    return pl.pallas_call(
        functools.partial(_scan_body, tc=tc, n=n, nch=nch),
        grid=(nb,),
        in_specs=[ispec(fwd), ispec(bwd), ispec(fwd), ispec(bwd), ispec(fwd), ispec(bwd)],
        out_specs=[ospec(fwd), ospec(bwd)],
        out_shape=[y_shape, y_shape],
        scratch_shapes=[pltpu.VMEM((n, n, LANES), F32), pltpu.VMEM((tc, 3, LANES, LANES), F32),
                        pltpu.VMEM((tc, n, LANES), F32)],
        compiler_params=_params(("arbitrary",)),
        name="wkv_scan",
    )(p1f, p1b, p2f, p2b, p3, p3)


def _post_body(yf_ref, yb_ref, bonus_ref, gate_ref, g_ref, b_ref, o_ref):
    yh = pltpu.einshape("thl->htl", yf_ref[...] + yb_ref[...])
    cols = []
    for h in range(N_HEADS_R):
        y = yh[h]
        mu = jnp.mean(y, axis=-1, keepdims=True)
        d = y - mu
        var = jnp.mean(d * d, axis=-1, keepdims=True)
        cols.append(d * lax.rsqrt(var + GN_EPS))
    yn = jnp.concatenate(cols, axis=-1) * g_ref[...] + b_ref[...]
    o_ref[...] = ((yn + bonus_ref[...]) * gate_ref[...]).astype(o_ref.dtype)


def rwkv_output(yf, yb, bonus, gate, ln_g, ln_b, B, L, Lc, with_ctx):
    T = bonus.shape[0]
    nlat = B * L
    tm = _tile(math.gcd(L, Lc), 256, 2 * SUBLANES)
    i0 = 0 if with_ctx else Lc // tm
    row = lambda b, i: _seq_row_block(b, i + i0, tm, L, Lc, nlat)
    y_spec = pl.BlockSpec((tm, N_HEADS_R, HEAD_R), lambda b, i: (i + i0, b, 0))
    tok_spec = pl.BlockSpec((tm, C_R), lambda b, i: (row(b, i), 0))
    vspec = pl.BlockSpec((1, C_R), lambda b, i: (0, 0))
    return pl.pallas_call(
        _post_body,
        grid=(B, (L + Lc) // tm - i0),
        in_specs=[y_spec, y_spec, tok_spec, tok_spec, vspec, vspec],
        out_specs=tok_spec,
        out_shape=jax.ShapeDtypeStruct((T, C_R), BF16),
        compiler_params=_params(("parallel", "parallel")),
        name="rwkv_output",
    )(yf, yb, bonus, gate, ln_g.reshape(1, C_R), ln_b.reshape(1, C_R))


def _attn_body(*refs, n_kv):
    q = refs[0][...]
    kv = refs[1:1 + 2 * n_kv]
    o_ref = refs[-1]
    s = [lax.dot_general(q, kv[2 * j][...], (((1,), (1,)), ((), ())), preferred_element_type=F32) for j in range(n_kv)]
    m = functools.reduce(jnp.maximum, [jnp.max(x, axis=-1, keepdims=True) for x in s])
    p = [jnp.exp2(x - m) for x in s]
    l = functools.reduce(jnp.add, [jnp.sum(x, axis=-1, keepdims=True) for x in p])
    o = functools.reduce(jnp.add, [jnp.dot(p[j].astype(BF16), kv[2 * j + 1][...], preferred_element_type=F32)
                                   for j in range(n_kv)])
    o_ref[...] = (o / l).astype(o_ref.dtype)


def mla_attention(q, kvs, rows_total, row_off, prev=None):
    B, H, Lq, dk = q.shape
    dv = kvs[0][1].shape[3]
    tq = _tile(Lq, 256, 2 * SUBLANES)
    nq = Lq // tq
    rbo = row_off // tq
    in_specs = [pl.BlockSpec((None, None, tq, dk), lambda b, h, i: (b, h, i, 0))]
    args = [q]
    for k, v in kvs:
        in_specs += [pl.BlockSpec((None, None, k.shape[2], dk), lambda b, h, i: (b, h, 0, 0)),
                     pl.BlockSpec((None, None, v.shape[2], dv), lambda b, h, i: (b, h, 0, 0))]
        args += [k, v]
    body = functools.partial(_attn_body, n_kv=len(kvs))
    aliases = {}
    if prev is not None:
        in_specs.append(pl.BlockSpec(memory_space=pl.ANY))
        args.append(prev)
        aliases = {len(args) - 1: 0}
        body = lambda *refs: _attn_body(*refs[:-2], refs[-1], n_kv=len(kvs))
    return pl.pallas_call(
        body,
        grid=(B, H, nq),
        in_specs=in_specs,
        out_specs=pl.BlockSpec((tq, dv), lambda b, h, i: (rbo + b * nq + i, h)),
        out_shape=jax.ShapeDtypeStruct((rows_total, H * dv), BF16),
        input_output_aliases=aliases,
        compiler_params=_params(("parallel", "parallel", "arbitrary")),
        name="mla_attention",
    )(*args)


def _moe_body(be_ref, nu_ref, x_ref, w1_ref, w3_ref, w2_ref, o_ref):
    blk = pl.program_id(0)
    f = pl.program_id(1)

    @pl.when(f == 0)
    def _():
        o_ref[...] = jnp.zeros_like(o_ref)

    @pl.when(blk < nu_ref[0])
    def _():
        x = x_ref[...]
        h1 = jnp.dot(x, w1_ref[...], preferred_element_type=F32)
        h3 = jnp.dot(x, w3_ref[...], preferred_element_type=F32)
        hb = (_silu(h1) * h3).astype(BF16)
        o_ref[...] += jnp.dot(hb, w2_ref[...], preferred_element_type=F32)


def moe_experts(xs, block_e, n_used, w1, w3, w2):
    P, D = xs.shape
    Fd = w1.shape[2]
    tf = _tile(Fd, 256, LANES)
    nf = Fd // tf
    nb = P // MOE_ROWS

    def fidx(i, f, nu):
        return jnp.where(i < nu[0], f, nf - 1)

    grid_spec = pltpu.PrefetchScalarGridSpec(
        num_scalar_prefetch=2,
        grid=(nb, nf),
        in_specs=[pl.BlockSpec((MOE_ROWS, D), lambda i, f, be, nu: (i, 0)),
                  pl.BlockSpec((None, D, tf), lambda i, f, be, nu: (be[i], 0, fidx(i, f, nu))),
                  pl.BlockSpec((None, D, tf), lambda i, f, be, nu: (be[i], 0, fidx(i, f, nu))),
                  pl.BlockSpec((None, tf, D), lambda i, f, be, nu: (be[i], fidx(i, f, nu), 0))],
        out_specs=pl.BlockSpec((MOE_ROWS, D), lambda i, f, be, nu: (i, 0)),
    )
    return pl.pallas_call(
        _moe_body,
        grid_spec=grid_spec,
        out_shape=jax.ShapeDtypeStruct((P, D), F32),
        compiler_params=_params(("arbitrary", "arbitrary")),
        name="moe_experts",
    )(block_e, n_used, xs, w1, w3, w2)


def _route(tok, w_router, router_bias):
    logits = jnp.dot(tok, w_router, precision=HP)
    scores = jax.nn.sigmoid(logits.astype(F32))
    biased = (scores + router_bias.astype(F32)).reshape(-1, N_GROUPS, EXPERTS_PER_GROUP)
    group_score = jnp.sum(lax.top_k(biased, 2)[0], -1)
    chosen = jnp.argmax(group_score, -1)
    in_group = jnp.arange(N_GROUPS)[None, :] == chosen[:, None]
    masked = jnp.where(in_group[:, :, None], biased, -jnp.inf).reshape(-1, N_EXPERTS)
    _, idx = lax.top_k(masked, TOP_K)
    w = jnp.take_along_axis(scores, idx, -1)
    return idx, w / jnp.sum(w, -1, keepdims=True)


def moe_ffn(tok, w_router, router_bias, w1, w3, w2):
    T, D = tok.shape
    idx, wts = _route(tok, w_router, router_bias)
    M = T * TOP_K
    flat_e = idx.reshape(-1).astype(jnp.int32)
    onehot = (flat_e[:, None] == jnp.arange(N_EXPERTS, dtype=jnp.int32)[None, :]).astype(jnp.int32)
    csum = jnp.cumsum(onehot, axis=0)
    rank = jnp.take_along_axis(csum, flat_e[:, None], axis=1)[:, 0] - 1
    counts = csum[-1]
    pcounts = (counts + MOE_ROWS - 1) // MOE_ROWS * MOE_ROWS
    pends = jnp.cumsum(pcounts)
    pstarts = pends - pcounts
    dest = pstarts[flat_e] + rank
    n_blocks = -(-M // MOE_ROWS) + N_EXPERTS
    P = n_blocks * MOE_ROWS
    flat_t = jnp.repeat(jnp.arange(T, dtype=jnp.int32), TOP_K)
    row_tok = jnp.zeros((P,), jnp.int32).at[dest].set(flat_t)
    n_used = (pends[-1] // MOE_ROWS).astype(jnp.int32)
    blk_start = jnp.arange(n_blocks, dtype=jnp.int32) * MOE_ROWS
    block_e = jnp.clip(jnp.searchsorted(pends, blk_start, side="right"), 0, N_EXPERTS - 1).astype(jnp.int32)
    last_e = block_e[jnp.maximum(n_used - 1, 0)]
    block_e = jnp.where(jnp.arange(n_blocks) < n_used, block_e, last_e)
    xs = tok.astype(BF16)[row_tok]
    yb = moe_experts(xs, block_e, n_used.reshape(1), w1, w3, w2)
    dest = dest.reshape(T, TOP_K)
    return sum(yb[dest[:, j]] * wts[:, j:j + 1] for j in range(TOP_K))


def _rope_tables(rows):
    half = ROPE // 2
    inv_freq = ROPE_BASE ** (-jnp.arange(0, half, 2, dtype=F32) / half)
    r = jnp.repeat(jnp.arange(rows, dtype=F32), GRID_W)
    cl = jnp.tile(jnp.arange(GRID_W, dtype=F32), rows)
    ar = r[:, None] * inv_freq
    ac = cl[:, None] * inv_freq
    ang = jnp.concatenate([ar, ar, ac, ac], -1)
    return jnp.cos(ang), jnp.sin(ang)


def _rot_cols(w):
    q = ROPE // 4
    return jnp.concatenate([-w[..., q:2 * q], w[..., :q], -w[..., 3 * q:], w[..., 2 * q:3 * q]], -1)


def _rot_perm(g):
    q = ROPE // 4
    return jnp.concatenate([g[q:2 * q], g[:q], g[3 * q:], g[2 * q:3 * q]], -1)


def _rope_tab(g, cos, sin):
    if cos is None:
        return jnp.concatenate([g, jnp.zeros_like(g)])[None, :]
    return jnp.concatenate([g[None, :] * cos, _rot_perm(g)[None, :] * sin], -1)


def _blockdiag2(w):
    z = jnp.zeros_like(w[0])
    return jnp.concatenate([jnp.concatenate([w[0], z], 1), jnp.concatenate([z, w[1]], 1)], 0)


def kernel(x, c, ctx, c_ctx, w_mod, b_mod, norm1_g, norm2_g, w_in, w_out, conv_dw, conv_b, conv_ln_g, conv_ln_b, r_mu, r_w0, r_w2, r_a0, r_a2, r_g2, r_kk, r_ka, r_rk, r_ln_g, r_ln_b, m_cq_g, m_w_uq, m_ckv_g, m_w_ukv, m_qn_g, m_qr_g, m_kn_g, m_kr_g, w_router, router_bias, moe_w1, moe_w3, moe_w2):
    B, L, D = x.shape
    Lc = ctx.shape[1]
    depth = w_mod.shape[0]
    nlat, nctx = B * L, B * Lc
    T = nlat + nctx
    cos, sin = _rope_tables(L // GRID_W)
    tm_in = _tile(math.gcd(L, nctx), 512, 2 * SUBLANES)
    bd = jnp.kron(jnp.eye(LANES // HEAD_R, dtype=F32), jnp.ones((HEAD_R, HEAD_R), F32))

    H = jnp.concatenate([x.reshape(nlat, D), ctx.reshape(nctx, D)], 0)
    crow = jnp.concatenate([c, c_ctx[None, :], jnp.zeros((SUBLANES - (B + 1) % SUBLANES, D), F32)], 0)

    for l in range(depth):
        last = l == depth - 1
        mod = adaln_mod(crow, w_mod, b_mod, l)[:B + 1]
        sh1, sc1, ga1, sh2, sc2, ga2 = jnp.split(mod, 6, axis=-1)
        n = norm_mod(H, norm1_g[l], sc1, sh1, L)

        wi = w_in[l]
        cols = lambda lo, hi: wi[:, lo:hi]
        w_conv = cast_bf16(w_in, l, OFF_CONV, OFF_MQ - OFF_CONV)
        w_mq = cast_bf16(w_in, l, OFF_MQ, OFF_RR - OFF_MQ)
        w_rw = cast_bf16(jnp.concatenate(
            [cols(OFF_RR, OFF_RG), cols(OFF_RK, OFF_RV), cols(OFF_RV, OFF_RW), cols(OFF_RW, OFF_RA),
             cols(OFF_RA, OFF_MKV), cols(OFF_RG, OFF_RK), jnp.zeros((D, G_PAD - G_LORA), F32)], 1))
        w_kv = cast_bf16(jnp.concatenate(
            [cols(OFF_MKV, OFF_MKR), cols(OFF_MKR, IN_COLS), _rot_cols(cols(OFF_MKR, IN_COLS))], 1))
        u = matmul_plain(n, w_conv, name="in_conv")
        pr = matmul_plain(n, w_rw, name="in_rwkv")
        cq = matmul(n, w_mq, epi=_epi_mq, tm=tm_in, tn=Q_LORA,
                    extra=(m_cq_g[l].reshape(1, Q_LORA),),
                    extra_specs=(pl.BlockSpec((1, Q_LORA), lambda i, j, k: (0, 0)),),
                    out_shapes=[jax.ShapeDtypeStruct((T, Q_LORA), BF16)],
                    out_specs=[pl.BlockSpec((tm_in, Q_LORA), lambda i, j, k: (i, 0))], name="in_mq")[0]
        ktab = jnp.concatenate([_rope_tab(m_kr_g[l], cos, sin),
                                jnp.broadcast_to(_rope_tab(m_kr_g[l], None, None), (tm_in, 2 * ROPE))], 0)
        nlb = L // tm_in
        ckv, kr = matmul(n, w_kv, epi=_epi_mkv, tm=tm_in, tn=KV_LORA + 2 * ROPE,
                         extra=(m_ckv_g[l].reshape(1, KV_LORA), ktab),
                         extra_specs=(pl.BlockSpec((1, KV_LORA), lambda i, j, k: (0, 0)),
                                      pl.BlockSpec((tm_in, 2 * ROPE),
                                                   lambda i, j, k: (jnp.where(i < B * nlb, i % nlb, nlb), 0))),
                         out_shapes=[jax.ShapeDtypeStruct((T, KV_LORA), BF16), jax.ShapeDtypeStruct((T, ROPE), F32)],
                         out_specs=[pl.BlockSpec((tm_in, KV_LORA), lambda i, j, k: (i, 0)),
                                    pl.BlockSpec((tm_in, ROPE), lambda i, j, k: (i, 0))], name="in_mkv")

        conv = conformer_conv(u, conv_dw[l], conv_b[l], conv_ln_g[l], conv_ln_b[l], B, L, Lc, not last)

        mu = r_mu[l]
        mcols = lambda lo, hi: mu[:, lo - OFF_RR:hi - OFF_RR]
        mu_r = jnp.concatenate([mcols(OFF_RR, OFF_RG), mcols(OFF_RK, OFF_RV), mcols(OFF_RV, OFF_RW),
                                mcols(OFF_RW, OFF_RA), mcols(OFF_RA, OFF_MKV), mcols(OFF_RG, OFF_RK),
                                jnp.zeros((2, G_PAD - G_LORA), F32)], 1)
        rp = {"mu_prev": mu_r[0:1], "mu_next": mu_r[1:2],
              "w0": r_w0[l].reshape(1, 2 * C_R), "w2": _blockdiag2(r_w2[l]),
              "a0": r_a0[l].reshape(1, 2 * C_R), "a2": _blockdiag2(r_a2[l]),
              "kk": r_kk[l].reshape(1, C_R), "ka": r_ka[l].reshape(1, C_R), "rk": r_rk[l].reshape(1, C_R),
              "g2": jnp.pad(r_g2[l], ((0, G_PAD - G_LORA), (0, 0))), "bd": bd}
        p1f, p1b, p2f, p2b, p3, bonus, gate = rwkv_features(pr, rp, B, L, Lc)
        yf, yb = wkv_scan(p1f, p1b, p2f, p2b, p3, Lc)
        rw = rwkv_output(yf, yb, bonus, gate, r_ln_g[l], r_ln_b[l], B, L, Lc, not last)

        wq = m_w_uq[l].reshape(Q_LORA, N_HEADS_M, NOPE + ROPE)
        wq = cast_bf16(jnp.concatenate([wq, _rot_cols(wq[..., NOPE:])], -1).reshape(Q_LORA, N_HEADS_M * QH))
        wkv = cast_bf16(m_w_ukv, l)
        k_c, v_c = mla_kv_up(ckv, wkv, m_kn_g[l], kr, B, Lc, nlat)
        k_l, v_l = mla_kv_up(ckv, wkv, m_kn_g[l], kr, B, L, 0)
        q_l = mla_q_up(cq, wq, m_qn_g[l], _rope_tab(m_qr_g[l], cos, sin), B, L, 0)
        rows_out = nlat if last else T
        mla = mla_attention(q_l, [(k_c, v_c), (k_l, v_l)], rows_out, 0)
        if not last:
            tq_c = _tile(Lc, 512, 2 * SUBLANES)
            q_c = mla_q_up(cq, wq, m_qn_g[l], jnp.broadcast_to(_rope_tab(m_qr_g[l], None, None), (tq_c, 2 * ROPE)),
                           B, Lc, nlat)
            mla = mla_attention(q_c, [(k_c, v_c)], rows_out, nlat, prev=mla)

        Hn = out_proj((conv, rw, mla), cast_bf16(w_out, l), H, ga1, rows_out, L)
        n2 = norm_mod(Hn, norm2_g[l], sc2, sh2, L, out_dtype=F32)
        y = moe_ffn(n2, w_router, router_bias, cast_bf16(moe_w1, l), cast_bf16(moe_w3, l), cast_bf16(moe_w2, l))
        grow = jnp.repeat(ga2[:B], L, axis=0)
        if not last:
            grow = jnp.concatenate([grow, jnp.repeat(ga2[B:], nctx, axis=0)], 0)
        H = Hn + grow * y
    return H[:nlat].reshape(B, L, D)
```

```python
import functools
import math

import jax
import jax.numpy as jnp
from jax import lax
from jax.experimental import pallas as pl
from jax.experimental.pallas import tpu as pltpu

GRID_W = 64
C_CONV = 1024
CONV_W = 31
N_HEADS_R = 16
HEAD_R = 64
C_R = N_HEADS_R * HEAD_R
W_LORA = 64
A_LORA = 64
G_LORA = 160
GN_EPS = 64e-5
N_HEADS_M = 16
NOPE = 128
ROPE = 64
V_HEAD = 128
Q_LORA = 1536
KV_LORA = 512
C_M = N_HEADS_M * V_HEAD
ROPE_BASE = 10000.0
ATTN_SCALE = (NOPE + ROPE) ** -0.5
Q_SCALE = ATTN_SCALE * math.log2(math.e)
RMS_EPS = 1e-6
LN_EPS = 1e-5

OFF_CONV = 0
OFF_MQ = OFF_CONV + 2 * C_CONV
OFF_RR = OFF_MQ + Q_LORA
OFF_RG = OFF_RR + C_R
OFF_RK = OFF_RG + G_LORA
OFF_RV = OFF_RK + C_R
OFF_RW = OFF_RV + C_R
OFF_RA = OFF_RW + 2 * W_LORA
OFF_MKV = OFF_RA + 2 * A_LORA
OFF_MKR = OFF_MKV + KV_LORA
IN_COLS = OFF_MKR + ROPE

N_EXPERTS = 16
N_GROUPS = 4
EXPERTS_PER_GROUP = N_EXPERTS // N_GROUPS
TOP_K = 2

LANES = 128
SUBLANES = 8
VMEM_LIMIT = 56 * 1024 * 1024
CONV_HALO = 16
MOE_ROWS = 512
CAST_BLOCK_ELEMS = 2 * 1024 * 1024
MM_ACC_ELEMS = 1024 * 1024
GATHER_ROWS = 256
MLA_HEAD_GROUP = 4
ATTN_SUBTILE = 256

RC_R, RC_K, RC_V, RC_W, RC_A, RC_G = 0, C_R, 2 * C_R, 3 * C_R, 3 * C_R + 2 * W_LORA, 3 * C_R + 2 * W_LORA + 2 * A_LORA
G_PAD = 2 * LANES
RCOLS = RC_G + G_PAD
QH = NOPE + 2 * ROPE
KD = NOPE + ROPE

BF16 = jnp.bfloat16
F32 = jnp.float32
HP = lax.Precision.HIGHEST


def _params(sem):
    return pltpu.CompilerParams(dimension_semantics=sem, vmem_limit_bytes=VMEM_LIMIT)


def _tile(n, target, mult):
    best = None
    for d in range(mult, min(n, target) + 1, mult):
        if n % d == 0:
            best = d
    return n if best is None else best


def _silu(x):
    return x * jax.nn.sigmoid(x)


def _rms(x, g):
    return x * lax.rsqrt(jnp.mean(x * x, axis=-1, keepdims=True) + RMS_EPS) * g


def _cast_body(x_ref, o_ref):
    o_ref[...] = x_ref[...].astype(o_ref.dtype)


def cast_bf16(w, layer=None, col_off=0, cols=None):
    w3 = w[None] if layer is None else w
    lay = 0 if layer is None else layer
    lead = w3.shape[1:-1]
    w3 = w3.reshape(w3.shape[0], -1, w3.shape[-1])
    R = w3.shape[1]
    C = w3.shape[2] - col_off if cols is None else cols
    tc = _tile(math.gcd(C, col_off) if col_off else C, 4096, LANES)
    tr = _tile(R, max(CAST_BLOCK_ELEMS // tc, 2 * SUBLANES), 2 * SUBLANES)
    cb = col_off // tc
    out = pl.pallas_call(
        _cast_body, grid=(R // tr, C // tc),
        in_specs=[pl.BlockSpec((None, tr, tc), lambda i, j: (lay, i, j + cb))],
        out_specs=pl.BlockSpec((tr, tc), lambda i, j: (i, j)),
        out_shape=jax.ShapeDtypeStruct((R, C), BF16),
        compiler_params=_params(("parallel", "parallel")), name="cast_bf16")(w3)
    return out.reshape(lead + (C,))


def _mod_body(c_ref, w_ref, b_ref, o_ref):
    @pl.when(pl.program_id(0) == 0)
    def _():
        o_ref[...] = jnp.broadcast_to(b_ref[...], o_ref.shape)

    a = _silu(c_ref[...]).astype(BF16)
    o_ref[...] += jnp.dot(a, w_ref[...].astype(BF16), preferred_element_type=F32)


def adaln_mod(rows, w, b, layer):
    R, D = rows.shape
    N = w.shape[2]
    tk = LANES
    return pl.pallas_call(
        _mod_body,
        grid=(D // tk,),
        in_specs=[pl.BlockSpec((R, tk), lambda k: (0, k)),
                  pl.BlockSpec((None, tk, N), lambda k: (layer, k, 0)),
                  pl.BlockSpec((None, 1, N), lambda k: (layer, 0, 0))],
        out_specs=pl.BlockSpec((R, N), lambda k: (0, 0)),
        out_shape=jax.ShapeDtypeStruct((R, N), F32),
        compiler_params=_params(("arbitrary",)),
        name="adaln_mod",
    )(rows, w, b.reshape(b.shape[0], 1, N))


def _norm_mod_body(x_ref, g_ref, sc_ref, sh_ref, o_ref):
    o_ref[...] = (_rms(x_ref[...], g_ref[...]) * (1.0 + sc_ref[...]) + sh_ref[...]).astype(o_ref.dtype)


def _norm_mod_router_body(x_ref, g_ref, sc_ref, sh_ref, wr_ref, o_ref, lg_ref):
    y = _rms(x_ref[...], g_ref[...]) * (1.0 + sc_ref[...]) + sh_ref[...]
    o_ref[...] = y.astype(o_ref.dtype)
    lg_ref[...] = jnp.dot(y, wr_ref[...], precision=HP, preferred_element_type=F32)


def norm_mod(h, g, sc, sh, rows_per_mod, w_router=None):
    T, D = h.shape
    R = sc.shape[0]
    tm = _tile(math.gcd(T, rows_per_mod), 256, 2 * SUBLANES)
    mod_row = lambda i: (jnp.minimum(i * tm // rows_per_mod, R - 1), 0, 0)
    in_specs = [pl.BlockSpec((tm, D), lambda i: (i, 0)),
                pl.BlockSpec((1, D), lambda i: (0, 0)),
                pl.BlockSpec((None, 1, D), mod_row),
                pl.BlockSpec((None, 1, D), mod_row)]
    args = [h, g.reshape(1, D), sc.reshape(R, 1, D), sh.reshape(R, 1, D)]
    o_spec = pl.BlockSpec((tm, D), lambda i: (i, 0))
    o_shape = jax.ShapeDtypeStruct((T, D), BF16 if w_router is None else F32)
    if w_router is None:
        return pl.pallas_call(_norm_mod_body, grid=(T // tm,), in_specs=in_specs, out_specs=o_spec, out_shape=o_shape,
                              compiler_params=_params(("parallel",)), name="norm_mod")(*args)
    wr = jnp.pad(w_router, ((0, 0), (0, LANES - w_router.shape[1])))
    return pl.pallas_call(
        _norm_mod_router_body, grid=(T // tm,),
        in_specs=in_specs + [pl.BlockSpec((D, LANES), lambda i: (0, 0))],
        out_specs=[o_spec, pl.BlockSpec((tm, LANES), lambda i: (i, 0))],
        out_shape=[o_shape, jax.ShapeDtypeStruct((T, LANES), F32)],
        compiler_params=_params(("parallel",)), name="norm_mod_router")(*args, wr)


def _mm_body(*refs, n_extra, n_out, epi):
    a_ref, b_ref = refs[0], refs[1]
    extra = refs[2:2 + n_extra]
    outs = refs[2 + n_extra:2 + n_extra + n_out]
    acc_ref = refs[-1]
    k = pl.program_id(2)

    @pl.when(k == 0)
    def _():
        acc_ref[...] = jnp.zeros_like(acc_ref)

    acc_ref[...] += jnp.dot(a_ref[...], b_ref[...], preferred_element_type=F32)

    @pl.when(k == pl.num_programs(2) - 1)
    def _():
        epi(acc_ref[...], extra, outs)


def matmul(a, b, *, epi, out_shapes, out_specs, tm, tn, tk=None, extra=(), extra_specs=(),
           n_row_blocks=None, row_block_off=0, name="matmul"):
    M, K = a.shape
    N = b.shape[1]
    tk = _tile(K, 1024, LANES) if tk is None else tk
    nrb = M // tm if n_row_blocks is None else n_row_blocks
    grid = (nrb, N // tn, K // tk)
    return pl.pallas_call(
        functools.partial(_mm_body, n_extra=len(extra), n_out=len(out_shapes), epi=epi),
        grid=grid,
        in_specs=[pl.BlockSpec((tm, tk), lambda i, j, k: (i + row_block_off, k)),
                  pl.BlockSpec((tk, tn), lambda i, j, k: (k, j))] + list(extra_specs),
        out_specs=list(out_specs), out_shape=list(out_shapes),
        scratch_shapes=[pltpu.VMEM((tm, tn), F32)],
        compiler_params=_params(("parallel", "parallel", "arbitrary")), name=name)(a, b, *extra)


def _epi_plain(acc, extra, outs):
    outs[0][...] = acc.astype(outs[0].dtype)


def matmul_plain(a, b, out_dtype=F32, name="matmul"):
    M, N = a.shape[0], b.shape[1]
    tn = _tile(N, 2048, 2 * LANES)
    tm = _tile(M, MM_ACC_ELEMS // tn, 2 * SUBLANES)
    return matmul(a, b, epi=_epi_plain, out_shapes=[jax.ShapeDtypeStruct((M, N), out_dtype)],
                  out_specs=[pl.BlockSpec((tm, tn), lambda i, j, k: (i, j))], tm=tm, tn=tn, name=name)[0]


def _rope_mix(t, tab):
    lane = lax.broadcasted_iota(jnp.int32, t.shape, 1)
    ss = jnp.sum(jnp.where(lane < ROPE, t * t, 0.0), axis=-1, keepdims=True)
    m = t * tab
    return (m + pltpu.roll(m, ROPE, axis=1)) * lax.rsqrt(ss * (1.0 / ROPE) + RMS_EPS)


def _epi_mq(acc, extra, outs):
    outs[0][...] = _rms(acc, extra[0][...]).astype(outs[0].dtype)


def _epi_mkv(acc, extra, outs):
    g_ref, tab_ref = extra
    outs[0][...] = _rms(acc[:, :KV_LORA], g_ref[...]).astype(outs[0].dtype)
    outs[1][...] = _rope_mix(acc[:, KV_LORA:], tab_ref[...])[:, :ROPE]


def _epi_qup(acc, extra, outs):
    g_ref, tab_ref = extra
    o = outs[0]
    for h in range(o.shape[0]):
        a = acc[:, h * QH:(h + 1) * QH]
        o[h, :, :NOPE] = (_rms(a[:, :NOPE], g_ref[...]) * Q_SCALE).astype(o.dtype)
        o[h, :, NOPE:] = (_rope_mix(a[:, NOPE:], tab_ref[...])[:, :ROPE] * Q_SCALE).astype(o.dtype)


def _epi_kvup(acc, extra, outs):
    g_ref, kr_ref = extra
    ok, ov = outs
    kr = kr_ref[...].astype(ok.dtype)
    for h in range(ok.shape[0]):
        a = acc[:, h * (NOPE + V_HEAD):(h + 1) * (NOPE + V_HEAD)]
        ok[h, :, :NOPE] = _rms(a[:, :NOPE], g_ref[...]).astype(ok.dtype)
        ok[h, :, NOPE:] = kr
        ov[h] = a[:, NOPE:].astype(ov.dtype)


def _epi_res(acc, extra, outs):
    res_ref, gate_ref = extra
    outs[0][...] = res_ref[...] + gate_ref[...] * acc


def mla_q_up(cq, w, qn_g, tab, B, Lx, row_off):
    tm = _tile(Lx, 512, 2 * SUBLANES)
    nt = Lx // tm
    hg = MLA_HEAD_GROUP
    return matmul(
        cq, w, epi=_epi_qup, tm=tm, tn=hg * QH, tk=Q_LORA,
        extra=(qn_g.reshape(1, NOPE), tab),
        extra_specs=(pl.BlockSpec((1, NOPE), lambda i, j, k: (0, 0)),
                     pl.BlockSpec((tm, 2 * ROPE), lambda i, j, k: (i % nt, 0))),
        out_shapes=[jax.ShapeDtypeStruct((B, N_HEADS_M, Lx, KD), BF16)],
        out_specs=[pl.BlockSpec((None, hg, tm, KD), lambda i, j, k: (i // nt, j, i % nt, 0))],
        n_row_blocks=B * nt, row_block_off=row_off // tm, name="mla_q_up")[0]


def mla_kv_up(ckv, w, kn_g, kr, B, Lx, row_off):
    tm = _tile(Lx, 512, 2 * SUBLANES)
    nt = Lx // tm
    rbo = row_off // tm
    hg = MLA_HEAD_GROUP
    return matmul(
        ckv, w, epi=_epi_kvup, tm=tm, tn=hg * (NOPE + V_HEAD), tk=KV_LORA,
        extra=(kn_g.reshape(1, NOPE), kr),
        extra_specs=(pl.BlockSpec((1, NOPE), lambda i, j, k: (0, 0)),
                     pl.BlockSpec((tm, ROPE), lambda i, j, k: (i + rbo, 0))),
        out_shapes=[jax.ShapeDtypeStruct((B, N_HEADS_M, Lx, KD), BF16),
                    jax.ShapeDtypeStruct((B, N_HEADS_M, Lx, V_HEAD), BF16)],
        out_specs=[pl.BlockSpec((None, hg, tm, KD), lambda i, j, k: (i // nt, j, i % nt, 0)),
                   pl.BlockSpec((None, hg, tm, V_HEAD), lambda i, j, k: (i // nt, j, i % nt, 0))],
        n_row_blocks=B * nt, row_block_off=rbo, name="mla_kv_up")


def _outproj_body(a0_ref, a1_ref, a2_ref, b_ref, res_ref, gate_ref, o_ref, acc_ref, *, bounds):
    k = pl.program_id(2)

    @pl.when(k == 0)
    def _():
        acc_ref[...] = jnp.zeros_like(acc_ref)

    for a_ref, (lo, hi) in zip((a0_ref, a1_ref, a2_ref), bounds):
        @pl.when((k >= lo) & (k < hi))
        def _():
            acc_ref[...] += jnp.dot(a_ref[...], b_ref[...], preferred_element_type=F32)

    @pl.when(k == pl.num_programs(2) - 1)
    def _():
        o_ref[...] = res_ref[...] + gate_ref[...] * acc_ref[...]


def out_proj(parts, w, res, gate, rows, rows_per_mod):
    N = w.shape[1]
    R = gate.shape[0]
    tm = _tile(math.gcd(rows, rows_per_mod), 1024, 2 * SUBLANES)
    tn = _tile(N, 1024, LANES)
    tk = _tile(math.gcd(*[p.shape[1] for p in parts]), 1024, LANES)
    bounds, lo = [], 0
    for p in parts:
        bounds.append((lo, lo + p.shape[1] // tk))
        lo = bounds[-1][1]

    def a_spec(b):
        return pl.BlockSpec((tm, tk), lambda i, j, k: (i, jnp.clip(k - b[0], 0, b[1] - b[0] - 1)))

    o_spec = pl.BlockSpec((tm, tn), lambda i, j, k: (i, j))
    return pl.pallas_call(
        functools.partial(_outproj_body, bounds=tuple(bounds)),
        grid=(rows // tm, N // tn, lo),
        in_specs=[a_spec(b) for b in bounds] + [
            pl.BlockSpec((tk, tn), lambda i, j, k: (k, j)), o_spec,
            pl.BlockSpec((None, 1, tn), lambda i, j, k: (jnp.minimum(i * tm // rows_per_mod, R - 1), 0, j))],
        out_specs=o_spec,
        out_shape=jax.ShapeDtypeStruct((rows, N), F32),
        scratch_shapes=[pltpu.VMEM((tm, tn), F32)],
        compiler_params=_params(("parallel", "parallel", "arbitrary")),
        name="out_proj")(*parts, w, res, gate.reshape(R, 1, N))


def _seq_row_block(b, i, t, L, Lc, nlat):
    nc = Lc // t
    return jnp.where(i < nc, (nlat + b * Lc) // t + i, (b * L) // t + (i - nc))


def _conv_body(u_ref, up_ref, un_ref, dw_ref, db_ref, g_ref, b_ref, o_ref, buf_ref, sh_ref, *, tl, rc, nc, i0):
    i = pl.program_id(1) + i0
    nt = pl.num_programs(1) + i0
    first_tile = (i == 0) | (i == nc)
    last_tile = (i == nc - 1) | (i == nt - 1)

    def glu(u):
        return u[:, :C_CONV] * jax.nn.sigmoid(u[:, C_CONV:])

    buf_ref[pl.ds(0, CONV_HALO), :] = jnp.where(first_tile, 0.0, glu(up_ref[...]))
    buf_ref[pl.ds(CONV_HALO, tl), :] = glu(u_ref[...])
    buf_ref[pl.ds(CONV_HALO + tl, CONV_HALO), :] = jnp.where(last_tile, 0.0, glu(un_ref[...]))
    nsh = tl + 2 * CONV_HALO - SUBLANES
    for s in range(1, SUBLANES):
        sh_ref[s - 1] = buf_ref[pl.ds(s, nsh), :]

    first = CONV_HALO - CONV_W // 2

    def chunk(c, carry):
        r0 = pl.multiple_of(c * rc, rc)
        acc = jnp.broadcast_to(db_ref[...], (rc, C_CONV))
        for k in range(CONV_W):
            q, s = divmod(first + k, SUBLANES)
            start = pl.multiple_of(r0 + q * SUBLANES, SUBLANES)
            win = buf_ref[pl.ds(start, rc), :] if s == 0 else sh_ref[s - 1, pl.ds(start, rc), :]
            acc = acc + dw_ref[pl.ds(k, 1), :] * win
        mu = jnp.mean(acc, axis=-1, keepdims=True)
        d = acc - mu
        var = jnp.mean(d * d, axis=-1, keepdims=True)
        y = d * lax.rsqrt(var + LN_EPS) * g_ref[...] + b_ref[...]
        o_ref[pl.ds(r0, rc), :] = _silu(y).astype(o_ref.dtype)
        return carry

    lax.fori_loop(0, tl // rc, chunk, 0)


def conformer_conv(u, dw, db, ln_g, ln_b, B, L, Lc, with_ctx):
    T = u.shape[0]
    nlat = B * L
    tl = _tile(math.gcd(L, Lc), 256, CONV_HALO)
    rc = _tile(tl, 32, SUBLANES)
    nc = Lc // tl
    i0 = 0 if with_ctx else nc
    hb = tl // CONV_HALO
    nhalo = T // CONV_HALO
    row = lambda b, i: _seq_row_block(b, i + i0, tl, L, Lc, nlat)
    body = functools.partial(_conv_body, tl=tl, rc=rc, nc=nc, i0=i0)
    vec = lambda a: a.reshape(1, C_CONV)
    vspec = pl.BlockSpec((1, C_CONV), lambda b, i: (0, 0))
    return pl.pallas_call(
        body,
        grid=(B, (L + Lc) // tl - i0),
        in_specs=[pl.BlockSpec((tl, 2 * C_CONV), lambda b, i: (row(b, i), 0)),
                  pl.BlockSpec((CONV_HALO, 2 * C_CONV), lambda b, i: (jnp.maximum(row(b, i) * hb - 1, 0), 0)),
                  pl.BlockSpec((CONV_HALO, 2 * C_CONV), lambda b, i: (jnp.minimum((row(b, i) + 1) * hb, nhalo - 1), 0)),
                  pl.BlockSpec((CONV_W, C_CONV), lambda b, i: (0, 0)),
                  vspec, vspec, vspec],
        out_specs=pl.BlockSpec((tl, C_CONV), lambda b, i: (row(b, i), 0)),
        out_shape=jax.ShapeDtypeStruct((T if with_ctx else nlat, C_CONV), BF16),
        scratch_shapes=[pltpu.VMEM((tl + 2 * CONV_HALO, C_CONV), F32),
                        pltpu.VMEM((SUBLANES - 1, tl + 2 * CONV_HALO - SUBLANES, C_CONV), F32)],
        compiler_params=_params(("parallel", "arbitrary")),
        name="conformer_conv",
    )(u, u, u, dw, vec(db), vec(ln_g), vec(ln_b))


def _softplus(z):
    return jnp.maximum(z, 0.0) + jnp.log(1.0 + jnp.exp(-jnp.abs(z)))


def _feat_body(pr_ref, prv_ref, nxt_ref, mup_ref, mun_ref, w0_ref, w2_ref, a0_ref, a2_ref, kkp_ref, ka_ref,
               rk_ref, g2_ref, bd_ref, p1f_ref, p1b_ref, p2f_ref, p2b_ref, p3_ref, bonus_ref, gate_ref, buf_ref,
               *, tm, nc):
    i = pl.program_id(1)
    nt = pl.num_programs(1)
    first_tile = (i == 0) | (i == nc)
    last_tile = (i == nc - 1) | (i == nt - 1)
    buf_ref[pl.ds(SUBLANES, tm), :] = pr_ref[...]
    buf_ref[pl.ds(0, SUBLANES), :] = jnp.where(first_tile, 0.0, prv_ref[...])
    buf_ref[pl.ds(SUBLANES + tm, SUBLANES), :] = jnp.where(last_tile, 0.0, nxt_ref[...])

    def shifted(lo, hi):
        p = buf_ref[pl.ds(SUBLANES, tm), lo:hi]
        pv = buf_ref[pl.ds(SUBLANES - 1, tm), lo:hi]
        nx = buf_ref[pl.ds(SUBLANES + 1, tm), lo:hi]
        return p + mup_ref[:, lo:hi] * (pv - p) + mun_ref[:, lo:hi] * (nx - p)

    def head_sum(x):
        return jnp.concatenate(
            [jnp.dot(x[:, c * LANES:(c + 1) * LANES], bd_ref[...], precision=HP, preferred_element_type=F32)
             for c in range(C_R // LANES)], axis=-1)

    r = shifted(RC_R, RC_K)
    k = shifted(RC_K, RC_V)
    v = shifted(RC_V, RC_W)
    wd = jnp.tanh(shifted(RC_W, RC_A))
    ad = shifted(RC_A, RC_G)
    rg = shifted(RC_G, RCOLS)
    wl = -_softplus(-(w0_ref[...] + jnp.dot(wd, w2_ref[...], precision=HP, preferred_element_type=F32))) - 0.5
    dec = jnp.exp(-jnp.exp(wl))
    a = jax.nn.sigmoid(a0_ref[...] + jnp.dot(ad, a2_ref[...], precision=HP, preferred_element_type=F32))
    kkv = k * kkp_ref[...]
    kk = kkv / jnp.maximum(jnp.sqrt(head_sum(kkv * kkv)), 1e-12)
    gate_ref[...] = jnp.dot(jax.nn.sigmoid(rg), g2_ref[...], precision=HP, preferred_element_type=F32)

    lane = lax.broadcasted_iota(jnp.int32, (tm, LANES), 1)
    low = lane < HEAD_R

    def pack(o_ref, x, y):
        heads = []
        for c in range(C_R // LANES):
            xa = x[:, c * LANES:(c + 1) * LANES]
            ya = y[:, c * LANES:(c + 1) * LANES]
            heads.append(jnp.where(low, xa, pltpu.roll(ya, HEAD_R, axis=1)))
            heads.append(jnp.where(low, pltpu.roll(xa, HEAD_R, axis=1), ya))
        o_ref[...] = pltpu.einshape("htl->thl", jnp.stack(heads, axis=0))

    bonus = jnp.zeros((tm, C_R), F32)
    for d, (o1, o2) in enumerate(((p1f_ref, p2f_ref), (p1b_ref, p2b_ref))):
        a_d = a[:, d * C_R:(d + 1) * C_R]
        k_d = k * (1.0 + (a_d - 1.0) * ka_ref[...])
        pack(o1, dec[:, d * C_R:(d + 1) * C_R], kk * a_d)
        pack(o2, k_d, kk)
        bonus = bonus + head_sum(r * k_d * rk_ref[...]) * v
    pack(p3_ref, v, r)
    bonus_ref[...] = bonus


def rwkv_features(pr, rp, B, L, Lc):
    T = pr.shape[0]
    nlat = B * L
    Lt = L + Lc
    tm = _tile(math.gcd(L, Lc), 256, SUBLANES)
    nc = Lc // tm
    nrow8 = T // SUBLANES
    hb = tm // SUBLANES
    row = lambda b, i: _seq_row_block(b, i, tm, L, Lc, nlat)
    full = lambda a: pl.BlockSpec(a.shape, lambda b, i: (0,) * a.ndim)
    consts = (rp["mu_prev"], rp["mu_next"], rp["w0"], rp["w2"], rp["a0"], rp["a2"], rp["kk"], rp["ka"], rp["rk"],
              rp["g2"], rp["bd"])
    pk_shape = jax.ShapeDtypeStruct((Lt, B * N_HEADS_R, 2 * HEAD_R), F32)
    pk_spec = pl.BlockSpec((tm, N_HEADS_R, 2 * HEAD_R), lambda b, i: (i, b, 0))
    tok_shape = jax.ShapeDtypeStruct((T, C_R), F32)
    tok_spec = pl.BlockSpec((tm, C_R), lambda b, i: (row(b, i), 0))
    return pl.pallas_call(
        functools.partial(_feat_body, tm=tm, nc=nc),
        grid=(B, Lt // tm),
        in_specs=[pl.BlockSpec((tm, RCOLS), lambda b, i: (row(b, i), 0)),
                  pl.BlockSpec((SUBLANES, RCOLS), lambda b, i: (jnp.maximum(row(b, i) * hb - 1, 0), 0)),
                  pl.BlockSpec((SUBLANES, RCOLS), lambda b, i: (jnp.minimum((row(b, i) + 1) * hb, nrow8 - 1), 0))]
                 + [full(a) for a in consts],
        out_specs=[pk_spec] * 5 + [tok_spec] * 2,
        out_shape=[pk_shape] * 5 + [tok_shape] * 2,
        scratch_shapes=[pltpu.VMEM((tm + 2 * SUBLANES, RCOLS), F32)],
        compiler_params=_params(("parallel", "arbitrary")),
        name="rwkv_features",
    )(pr, pr, pr, *consts)


def _fold8(parts):
    sub = lax.broadcasted_iota(jnp.int32, parts[0].shape, 0)
    d = 1
    while len(parts) > 1:
        m = (sub // d) % 2 == 0
        parts = [jnp.where(m, a, b) + pltpu.roll(jnp.where(m, b, a), d, axis=0)
                 for a, b in zip(parts[0::2], parts[1::2])]
        d *= 2
    return parts[0]


def _scan_body(p1f_ref, p1b_ref, p2f_ref, p2b_ref, p3f_ref, p3b_ref, yf_ref, yb_ref, s_ref, op_ref, ybuf_ref,
               *, tc, n, nch):
    @pl.when(pl.program_id(0) == 0)
    def _():
        s_ref[...] = jnp.zeros_like(s_ref)

    def prep(t, g_prev):
        tb = tc - 1 - t
        zt = [jnp.concatenate([f_ref[t], b_ref[tb]], axis=0).T
              for f_ref, b_ref in ((p1f_ref, p1b_ref), (p2f_ref, p2b_ref), (p3f_ref, p3b_ref))]
        g = g_prev * zt[0][:n]
        inv = 1.0 / g
        op_ref[t, 0] = zt[1][n:] * g_prev
        op_ref[t, 1] = zt[0][n:] * inv
        op_ref[t, 2] = zt[1][:n] * inv
        op_ref[t, 3] = zt[2][n:] * g
        op_ref[t, 4] = zt[2][:n]
        return g

    g_last = lax.fori_loop(0, tc, prep, jnp.ones((n, LANES), F32), unroll=4)
    nv = n // SUBLANES

    def colsum(x):
        return jnp.sum(x.reshape(nv, SUBLANES, x.shape[-1]), axis=0)

    def step(t, carry):
        kkg = op_ref[t, 0]
        kkai = op_ref[t, 1]
        ki = op_ref[t, 2]
        rg = op_ref[t, 3]

        def rows(ib, c2):
            base = pl.multiple_of(ib * SUBLANES, SUBLANES)
            sa8 = _fold8([colsum(s_ref[base + ii] * kkg) for ii in range(SUBLANES)])
            v8 = op_ref[t, 4, pl.ds(base, SUBLANES), :]
            py = []
            for ii in range(SUBLANES):
                s_new = s_ref[base + ii] - sa8[ii:ii + 1] * kkai + v8[ii:ii + 1] * ki
                s_ref[base + ii] = s_new
                py.append(colsum(s_new * rg))
            ybuf_ref[t, pl.ds(base, SUBLANES), :] = _fold8(py)
            return c2

        lax.fori_loop(0, nv, rows, 0)
        return carry

    lax.fori_loop(0, tc, step, 0)

    def rescale(i, carry):
        s_ref[i] = s_ref[i] * g_last
        return carry

    lax.fori_loop(0, n, rescale, 0, unroll=4)

    def fin(t, carry):
        y = ybuf_ref[t]
        yt = jnp.concatenate([y, y], axis=0).T
        yf_ref[t] = yt[:nch, :n]
        yb_ref[tc - 1 - t] = yt[nch:, :n]
        return carry

    lax.fori_loop(0, tc, fin, 0, unroll=4)


def wkv_scan(p1f, p1b, p2f, p2b, p3, Lc):
    Lt, nch, _ = p1f.shape
    n = HEAD_R
    assert 2 * nch == LANES and 2 * n == LANES
    tc = _tile(math.gcd(Lc, Lt - Lc), 32, 1)
    ncb, nb = Lc // tc, Lt // tc
    fwd = lambda g: (g, 0, 0)
    bwd = lambda g: (jnp.where(g < ncb, ncb - 1 - g, nb - 1 - (g - ncb)), 0, 0)
    ispec = lambda m: pl.BlockSpec((tc, nch, 2 * n), m)
    ospec = lambda m: pl.BlockSpec((tc, nch, n), m)
    y_shape = jax.ShapeDtypeStruct((Lt, nch, n), F32)
    return pl.pallas_call(
        functools.partial(_scan_body, tc=tc, n=n, nch=nch),
        grid=(nb,),
        in_specs=[ispec(fwd), ispec(bwd), ispec(fwd), ispec(bwd), ispec(fwd), ispec(bwd)],
        out_specs=[ospec(fwd), ospec(bwd)],
        out_shape=[y_shape, y_shape],
        scratch_shapes=[pltpu.VMEM((n, n, LANES), F32), pltpu.VMEM((tc, 5, n, LANES), F32),
                        pltpu.VMEM((tc, n, LANES), F32)],
        compiler_params=_params(("arbitrary",)),
        name="wkv_scan",
    )(p1f, p1b, p2f, p2b, p3, p3)


def _post_body(yf_ref, yb_ref, bonus_ref, gate_ref, g_ref, b_ref, o_ref):
    yh = pltpu.einshape("thl->htl", yf_ref[...] + yb_ref[...])
    cols = []
    for h in range(N_HEADS_R):
        y = yh[h]
        mu = jnp.mean(y, axis=-1, keepdims=True)
        d = y - mu
        var = jnp.mean(d * d, axis=-1, keepdims=True)
        cols.append(d * lax.rsqrt(var + GN_EPS))
    yn = jnp.concatenate(cols, axis=-1) * g_ref[...] + b_ref[...]
    o_ref[...] = ((yn + bonus_ref[...]) * gate_ref[...]).astype(o_ref.dtype)


def rwkv_output(yf, yb, bonus, gate, ln_g, ln_b, B, L, Lc, with_ctx):
    T = bonus.shape[0]
    nlat = B * L
    tm = _tile(math.gcd(L, Lc), 256, 2 * SUBLANES)
    i0 = 0 if with_ctx else Lc // tm
    row = lambda b, i: _seq_row_block(b, i + i0, tm, L, Lc, nlat)
    y_spec = pl.BlockSpec((tm, N_HEADS_R, HEAD_R), lambda b, i: (i + i0, b, 0))
    tok_spec = pl.BlockSpec((tm, C_R), lambda b, i: (row(b, i), 0))
    vspec = pl.BlockSpec((1, C_R), lambda b, i: (0, 0))
    return pl.pallas_call(
        _post_body,
        grid=(B, (L + Lc) // tm - i0),
        in_specs=[y_spec, y_spec, tok_spec, tok_spec, vspec, vspec],
        out_specs=tok_spec,
        out_shape=jax.ShapeDtypeStruct((T if with_ctx else nlat, C_R), BF16),
        compiler_params=_params(("parallel", "parallel")),
        name="rwkv_output",
    )(yf, yb, bonus, gate, ln_g.reshape(1, C_R), ln_b.reshape(1, C_R))


def _attn_body(*refs, n_kv, hq):
    kv = refs[1:1 + 2 * n_kv]
    o_ref = refs[-1]
    for a in range(refs[0].shape[0] // hq):
        q = refs[0][pl.ds(a * hq, hq), :]
        s = [lax.dot_general(q, kv[2 * j][...], (((1,), (1,)), ((), ())), preferred_element_type=F32)
             for j in range(n_kv)]
        m = functools.reduce(jnp.maximum, [jnp.max(x, axis=-1, keepdims=True) for x in s])
        p = [jnp.exp2(x - m) for x in s]
        l = functools.reduce(jnp.add, [jnp.sum(x, axis=-1, keepdims=True) for x in p])
        o = functools.reduce(jnp.add, [jnp.dot(p[j].astype(BF16), kv[2 * j + 1][...], preferred_element_type=F32)
                                       for j in range(n_kv)])
        o_ref[pl.ds(a * hq, hq), :] = (o / l).astype(o_ref.dtype)


def mla_attention(q, kvs, rows_total, row_off, prev=None):
    B, H, Lq, dk = q.shape
    dv = kvs[0][1].shape[3]
    hq = _tile(Lq, ATTN_SUBTILE, 2 * SUBLANES)
    tq = _tile(Lq, 2 * hq, hq)
    nq = Lq // tq
    rbo = row_off // tq
    in_specs = [pl.BlockSpec((None, None, tq, dk), lambda b, h, i: (b, h, i, 0))]
    args = [q]
    for k, v in kvs:
        in_specs += [pl.BlockSpec((None, None, k.shape[2], dk), lambda b, h, i: (b, h, 0, 0)),
                     pl.BlockSpec((None, None, v.shape[2], dv), lambda b, h, i: (b, h, 0, 0))]
        args += [k, v]
    body = functools.partial(_attn_body, n_kv=len(kvs), hq=hq)
    aliases = {}
    if prev is not None:
        in_specs.append(pl.BlockSpec(memory_space=pl.ANY))
        args.append(prev)
        aliases = {len(args) - 1: 0}
        body = lambda *refs: _attn_body(*refs[:-2], refs[-1], n_kv=len(kvs), hq=hq)
    return pl.pallas_call(
        body,
        grid=(B, H, nq),
        in_specs=in_specs,
        out_specs=pl.BlockSpec((tq, dv), lambda b, h, i: (rbo + b * nq + i, h)),
        out_shape=jax.ShapeDtypeStruct((rows_total, H * dv), BF16),
        input_output_aliases=aliases,
        compiler_params=_params(("parallel", "parallel", "arbitrary")),
        name="mla_attention",
    )(*args)


def _moe_body(be_ref, nu_ref, x_ref, w1_ref, w3_ref, w2_ref, o_ref):
    blk = pl.program_id(0)
    f = pl.program_id(1)

    @pl.when(f == 0)
    def _():
        o_ref[...] = jnp.zeros_like(o_ref)

    @pl.when(blk < nu_ref[0])
    def _():
        x = x_ref[...]
        h1 = jnp.dot(x, w1_ref[...], preferred_element_type=F32)
        h3 = jnp.dot(x, w3_ref[...], preferred_element_type=F32)
        hb = (_silu(h1) * h3).astype(BF16)
        o_ref[...] += jnp.dot(hb, w2_ref[...], preferred_element_type=F32)


def moe_experts(xs, block_e, n_used, w1, w3, w2):
    P, D = xs.shape
    Fd = w1.shape[2]
    tf = _tile(Fd, 256, LANES)
    nf = Fd // tf
    nb = P // MOE_ROWS

    def fidx(i, f, nu):
        return jnp.where(i < nu[0], f, nf - 1)

    grid_spec = pltpu.PrefetchScalarGridSpec(
        num_scalar_prefetch=2,
        grid=(nb, nf),
        in_specs=[pl.BlockSpec((MOE_ROWS, D), lambda i, f, be, nu: (i, 0)),
                  pl.BlockSpec((None, D, tf), lambda i, f, be, nu: (be[i], 0, fidx(i, f, nu))),
                  pl.BlockSpec((None, D, tf), lambda i, f, be, nu: (be[i], 0, fidx(i, f, nu))),
                  pl.BlockSpec((None, tf, D), lambda i, f, be, nu: (be[i], fidx(i, f, nu), 0))],
        out_specs=pl.BlockSpec((MOE_ROWS, D), lambda i, f, be, nu: (i, 0)),
    )
    return pl.pallas_call(
        _moe_body,
        grid_spec=grid_spec,
        out_shape=jax.ShapeDtypeStruct((P, D), F32),
        compiler_params=_params(("arbitrary", "arbitrary")),
        name="moe_experts",
    )(block_e, n_used, xs, w1, w3, w2)


def _gather_body(*refs, n_src, tm, combine):
    idx_refs = refs[:n_src]
    src_ref = refs[n_src]
    rest = refs[n_src + 1:]
    if combine:
        w_ref, base_ref, gate_ref, o_ref, buf_ref, sem_ref = rest
    else:
        o_ref, buf_ref, sem_ref = rest
    i = pl.program_id(0)
    nsteps = pl.num_programs(0)
    slot = i % 2

    def row_copy(step, r, j, sl):
        row = idx_refs[j][step * tm + r]
        return pltpu.make_async_copy(src_ref.at[pl.ds(row, 1), :], buf_ref.at[sl, j, pl.ds(r, 1), :], sem_ref.at[sl])

    def start_all(step, sl):
        def body(r, c):
            for j in range(n_src):
                row_copy(step, r, j, sl).start()
            return c
        lax.fori_loop(0, tm, body, 0, unroll=8)

    def wait_all(step, sl):
        def body(r, c):
            for j in range(n_src):
                row_copy(step, r, j, sl).wait()
            return c
        lax.fori_loop(0, tm, body, 0, unroll=8)

    @pl.when(i == 0)
    def _():
        start_all(0, 0)

    @pl.when(i + 1 < nsteps)
    def _():
        start_all(i + 1, 1 - slot)

    wait_all(i, slot)
    if combine:
        acc = buf_ref[slot, 0] * w_ref[:, 0:1]
        for j in range(1, n_src):
            acc = acc + buf_ref[slot, j] * w_ref[:, j:j + 1]
        o_ref[...] = base_ref[...] + gate_ref[...] * acc
    else:
        o_ref[...] = buf_ref[slot, 0].astype(o_ref.dtype)


def gather_rows(src, idx, out_dtype):
    R = idx.shape[0]
    D = src.shape[1]
    tm = _tile(R, GATHER_ROWS, 2 * SUBLANES)
    grid_spec = pltpu.PrefetchScalarGridSpec(
        num_scalar_prefetch=1, grid=(R // tm,),
        in_specs=[pl.BlockSpec(memory_space=pl.ANY)],
        out_specs=pl.BlockSpec((tm, D), lambda i, ix: (i, 0)),
        scratch_shapes=[pltpu.VMEM((2, 1, tm, D), F32), pltpu.SemaphoreType.DMA((2,))])
    return pl.pallas_call(
        functools.partial(_gather_body, n_src=1, tm=tm, combine=False), grid_spec=grid_spec,
        out_shape=jax.ShapeDtypeStruct((R, D), out_dtype),
        compiler_params=_params(("arbitrary",)), name="moe_dispatch")(idx, src)


def combine_rows(src, idxs, w, base, gate, rows_per_mod):
    T, D = base.shape
    n_src = len(idxs)
    Rg = gate.shape[0]
    tm = _tile(math.gcd(T, rows_per_mod), GATHER_ROWS // n_src, SUBLANES)
    grid_spec = pltpu.PrefetchScalarGridSpec(
        num_scalar_prefetch=n_src, grid=(T // tm,),
        in_specs=[pl.BlockSpec(memory_space=pl.ANY),
                  pl.BlockSpec((tm, n_src), lambda i, *ix: (i, 0)),
                  pl.BlockSpec((tm, D), lambda i, *ix: (i, 0)),
                  pl.BlockSpec((None, 1, D), lambda i, *ix: (jnp.minimum(i * tm // rows_per_mod, Rg - 1), 0, 0))],
        out_specs=pl.BlockSpec((tm, D), lambda i, *ix: (i, 0)),
        scratch_shapes=[pltpu.VMEM((2, n_src, tm, D), F32), pltpu.SemaphoreType.DMA((2,))])
    return pl.pallas_call(
        functools.partial(_gather_body, n_src=n_src, tm=tm, combine=True), grid_spec=grid_spec,
        out_shape=jax.ShapeDtypeStruct((T, D), F32),
        compiler_params=_params(("arbitrary",)), name="moe_combine")(*idxs, src, w, base, gate.reshape(Rg, 1, D))


def _top2(x):
    pos = lax.broadcasted_iota(jnp.int32, x.shape, x.ndim - 1)
    i1 = jnp.argmax(x, -1)
    m1 = jnp.max(x, -1)
    x2 = jnp.where(pos == i1[..., None], -jnp.inf, x)
    return jnp.stack([m1, jnp.max(x2, -1)], -1), jnp.stack([i1, jnp.argmax(x2, -1)], -1)


def _route(logits, router_bias):
    scores = jax.nn.sigmoid(logits)
    biased = (scores + router_bias.astype(F32)).reshape(-1, N_GROUPS, EXPERTS_PER_GROUP)
    group_score = jnp.sum(_top2(biased)[0], -1)
    chosen = jnp.argmax(group_score, -1)
    in_group = jnp.arange(N_GROUPS)[None, :] == chosen[:, None]
    masked = jnp.where(in_group[:, :, None], biased, -jnp.inf).reshape(-1, N_EXPERTS)
    _, idx = _top2(masked)
    w = jnp.take_along_axis(scores, idx, -1)
    return idx, w / jnp.sum(w, -1, keepdims=True)


def moe_ffn(tok, logits, router_bias, w1, w3, w2, base, gate, rows_per_mod):
    T, D = tok.shape
    idx, wts = _route(logits, router_bias)
    M = T * TOP_K
    flat_e = idx.reshape(-1).astype(jnp.int32)
    onehot = (flat_e[:, None] == jnp.arange(N_EXPERTS, dtype=jnp.int32)[None, :]).astype(jnp.int32)
    csum = jnp.cumsum(onehot, axis=0)
    rank = jnp.take_along_axis(csum, flat_e[:, None], axis=1)[:, 0] - 1
    counts = csum[-1]
    pcounts = (counts + MOE_ROWS - 1) // MOE_ROWS * MOE_ROWS
    pends = jnp.cumsum(pcounts)
    pstarts = pends - pcounts
    dest = pstarts[flat_e] + rank
    n_blocks = -(-M // MOE_ROWS) + N_EXPERTS
    P = n_blocks * MOE_ROWS
    flat_t = jnp.repeat(jnp.arange(T, dtype=jnp.int32), TOP_K)
    row_tok = jnp.zeros((P,), jnp.int32).at[dest].set(flat_t)
    n_used = (pends[-1] // MOE_ROWS).astype(jnp.int32)
    blk_start = jnp.arange(n_blocks, dtype=jnp.int32) * MOE_ROWS
    block_e = jnp.clip(jnp.searchsorted(pends, blk_start, side="right"), 0, N_EXPERTS - 1).astype(jnp.int32)
    last_e = block_e[jnp.maximum(n_used - 1, 0)]
    block_e = jnp.where(jnp.arange(n_blocks) < n_used, block_e, last_e)
    xs = gather_rows(tok, row_tok, BF16)
    yb = moe_experts(xs, block_e, n_used.reshape(1), w1, w3, w2)
    dest = dest.reshape(T, TOP_K)
    return combine_rows(yb, [dest[:, j] for j in range(TOP_K)], wts, base, gate, rows_per_mod)


def _rope_tables(rows):
    half = ROPE // 2
    inv_freq = ROPE_BASE ** (-jnp.arange(0, half, 2, dtype=F32) / half)
    r = jnp.repeat(jnp.arange(rows, dtype=F32), GRID_W)
    cl = jnp.tile(jnp.arange(GRID_W, dtype=F32), rows)
    ar = r[:, None] * inv_freq
    ac = cl[:, None] * inv_freq
    ang = jnp.concatenate([ar, ar, ac, ac], -1)
    return jnp.cos(ang), jnp.sin(ang)


def _rot_cols(w):
    q = ROPE // 4
    return jnp.concatenate([-w[..., q:2 * q], w[..., :q], -w[..., 3 * q:], w[..., 2 * q:3 * q]], -1)


def _rot_perm(g):
    q = ROPE // 4
    return jnp.concatenate([g[q:2 * q], g[:q], g[3 * q:], g[2 * q:3 * q]], -1)


def _rope_tab(g, cos, sin):
    if cos is None:
        return jnp.concatenate([g, jnp.zeros_like(g)])[None, :]
    return jnp.concatenate([g[None, :] * cos, _rot_perm(g)[None, :] * sin], -1)


def _blockdiag2(w):
    z = jnp.zeros_like(w[0])
    return jnp.concatenate([jnp.concatenate([w[0], z], 1), jnp.concatenate([z, w[1]], 1)], 0)


def kernel(x, c, ctx, c_ctx, w_mod, b_mod, norm1_g, norm2_g, w_in, w_out, conv_dw, conv_b, conv_ln_g, conv_ln_b, r_mu, r_w0, r_w2, r_a0, r_a2, r_g2, r_kk, r_ka, r_rk, r_ln_g, r_ln_b, m_cq_g, m_w_uq, m_ckv_g, m_w_ukv, m_qn_g, m_qr_g, m_kn_g, m_kr_g, w_router, router_bias, moe_w1, moe_w3, moe_w2):
    B, L, D = x.shape
    Lc = ctx.shape[1]
    depth = w_mod.shape[0]
    nlat, nctx = B * L, B * Lc
    T = nlat + nctx
    cos, sin = _rope_tables(L // GRID_W)
    tm_in = _tile(math.gcd(L, nctx), 512, 2 * SUBLANES)
    bd = jnp.kron(jnp.eye(LANES // HEAD_R, dtype=F32), jnp.ones((HEAD_R, HEAD_R), F32))

    H = jnp.concatenate([x.reshape(nlat, D), ctx.reshape(nctx, D)], 0)
    crow = jnp.concatenate([c, c_ctx[None, :], jnp.zeros((SUBLANES - (B + 1) % SUBLANES, D), F32)], 0)

    for l in range(depth):
        last = l == depth - 1
        mod = adaln_mod(crow, w_mod, b_mod, l)[:B + 1]
        sh1, sc1, ga1, sh2, sc2, ga2 = jnp.split(mod, 6, axis=-1)
        n = norm_mod(H, norm1_g[l], sc1, sh1, L)

        wi = w_in[l]
        cols = lambda lo, hi: wi[:, lo:hi]
        w_conv = cast_bf16(w_in, l, OFF_CONV, OFF_MQ - OFF_CONV)
        w_mq = cast_bf16(w_in, l, OFF_MQ, OFF_RR - OFF_MQ)
        w_rw = cast_bf16(jnp.concatenate(
            [cols(OFF_RR, OFF_RG), cols(OFF_RK, OFF_RV), cols(OFF_RV, OFF_RW), cols(OFF_RW, OFF_RA),
             cols(OFF_RA, OFF_MKV), cols(OFF_RG, OFF_RK), jnp.zeros((D, G_PAD - G_LORA), F32)], 1))
        w_kv = cast_bf16(jnp.concatenate(
            [cols(OFF_MKV, OFF_MKR), cols(OFF_MKR, IN_COLS), _rot_cols(cols(OFF_MKR, IN_COLS))], 1))
        u = matmul_plain(n, w_conv, name="in_conv")
        pr = matmul_plain(n, w_rw, name="in_rwkv")
        cq = matmul(n, w_mq, epi=_epi_mq, tm=tm_in, tn=Q_LORA,
                    extra=(m_cq_g[l].reshape(1, Q_LORA),),
                    extra_specs=(pl.BlockSpec((1, Q_LORA), lambda i, j, k: (0, 0)),),
                    out_shapes=[jax.ShapeDtypeStruct((T, Q_LORA), BF16)],
                    out_specs=[pl.BlockSpec((tm_in, Q_LORA), lambda i, j, k: (i, 0))], name="in_mq")[0]
        ktab = jnp.concatenate([_rope_tab(m_kr_g[l], cos, sin),
                                jnp.broadcast_to(_rope_tab(m_kr_g[l], None, None), (tm_in, 2 * ROPE))], 0)
        nlb = L // tm_in
        ckv, kr = matmul(n, w_kv, epi=_epi_mkv, tm=tm_in, tn=KV_LORA + 2 * ROPE,
                         extra=(m_ckv_g[l].reshape(1, KV_LORA), ktab),
                         extra_specs=(pl.BlockSpec((1, KV_LORA), lambda i, j, k: (0, 0)),
                                      pl.BlockSpec((tm_in, 2 * ROPE),
                                                   lambda i, j, k: (jnp.where(i < B * nlb, i % nlb, nlb), 0))),
                         out_shapes=[jax.ShapeDtypeStruct((T, KV_LORA), BF16), jax.ShapeDtypeStruct((T, ROPE), F32)],
                         out_specs=[pl.BlockSpec((tm_in, KV_LORA), lambda i, j, k: (i, 0)),
                                    pl.BlockSpec((tm_in, ROPE), lambda i, j, k: (i, 0))], name="in_mkv")

        conv = conformer_conv(u, conv_dw[l], conv_b[l], conv_ln_g[l], conv_ln_b[l], B, L, Lc, not last)

        mu = r_mu[l]
        mcols = lambda lo, hi: mu[:, lo - OFF_RR:hi - OFF_RR]
        mu_r = jnp.concatenate([mcols(OFF_RR, OFF_RG), mcols(OFF_RK, OFF_RV), mcols(OFF_RV, OFF_RW),
                                mcols(OFF_RW, OFF_RA), mcols(OFF_RA, OFF_MKV), mcols(OFF_RG, OFF_RK),
                                jnp.zeros((2, G_PAD - G_LORA), F32)], 1)
        rp = {"mu_prev": mu_r[0:1], "mu_next": mu_r[1:2],
              "w0": r_w0[l].reshape(1, 2 * C_R), "w2": _blockdiag2(r_w2[l]),
              "a0": r_a0[l].reshape(1, 2 * C_R), "a2": _blockdiag2(r_a2[l]),
              "kk": r_kk[l].reshape(1, C_R), "ka": r_ka[l].reshape(1, C_R), "rk": r_rk[l].reshape(1, C_R),
              "g2": jnp.pad(r_g2[l], ((0, G_PAD - G_LORA), (0, 0))), "bd": bd}
        p1f, p1b, p2f, p2b, p3, bonus, gate = rwkv_features(pr, rp, B, L, Lc)
        yf, yb = wkv_scan(p1f, p1b, p2f, p2b, p3, Lc)
        rw = rwkv_output(yf, yb, bonus, gate, r_ln_g[l], r_ln_b[l], B, L, Lc, not last)

        wq = m_w_uq[l].reshape(Q_LORA, N_HEADS_M, NOPE + ROPE)
        wq = cast_bf16(jnp.concatenate([wq, _rot_cols(wq[..., NOPE:])], -1).reshape(Q_LORA, N_HEADS_M * QH))
        wkv = cast_bf16(m_w_ukv, l)
        k_c, v_c = mla_kv_up(ckv, wkv, m_kn_g[l], kr, B, Lc, nlat)
        k_l, v_l = mla_kv_up(ckv, wkv, m_kn_g[l], kr, B, L, 0)
        q_l = mla_q_up(cq, wq, m_qn_g[l], _rope_tab(m_qr_g[l], cos, sin), B, L, 0)
        rows_out = nlat if last else T
        mla = mla_attention(q_l, [(k_c, v_c), (k_l, v_l)], rows_out, 0,
                            prev=None if last else jnp.zeros((T, C_M), BF16))
        if not last:
            tq_c = _tile(Lc, 512, 2 * SUBLANES)
            q_c = mla_q_up(cq, wq, m_qn_g[l], jnp.broadcast_to(_rope_tab(m_qr_g[l], None, None), (tq_c, 2 * ROPE)),
                           B, Lc, nlat)
            mla = mla_attention(q_c, [(k_c, v_c)], rows_out, nlat, prev=mla)

        Hn = out_proj((conv, rw, mla), cast_bf16(w_out, l), H, ga1, rows_out, L)
        n2, logits = norm_mod(Hn, norm2_g[l], sc2, sh2, L, w_router=w_router)
        H = moe_ffn(n2, logits[:, :N_EXPERTS], router_bias,
                    cast_bf16(moe_w1, l), cast_bf16(moe_w3, l), cast_bf16(moe_w2, l), Hn, ga2, L)
    return H[:nlat].reshape(B, L, D)
```

```python
import functools
import math

import jax
import jax.numpy as jnp
from jax import lax
from jax.experimental import pallas as pl
from jax.experimental.pallas import tpu as pltpu

GRID_W = 64
C_CONV = 1024
CONV_W = 31
N_HEADS_R = 16
HEAD_R = 64
C_R = N_HEADS_R * HEAD_R
W_LORA = 64
A_LORA = 64
G_LORA = 160
GN_EPS = 64e-5
N_HEADS_M = 16
NOPE = 128
ROPE = 64
V_HEAD = 128
Q_LORA = 1536
KV_LORA = 512
C_M = N_HEADS_M * V_HEAD
ROPE_BASE = 10000.0
ATTN_SCALE = (NOPE + ROPE) ** -0.5
Q_SCALE = ATTN_SCALE * math.log2(math.e)
RMS_EPS = 1e-6
LN_EPS = 1e-5

OFF_CONV = 0
OFF_MQ = OFF_CONV + 2 * C_CONV
OFF_RR = OFF_MQ + Q_LORA
OFF_RG = OFF_RR + C_R
OFF_RK = OFF_RG + G_LORA
OFF_RV = OFF_RK + C_R
OFF_RW = OFF_RV + C_R
OFF_RA = OFF_RW + 2 * W_LORA
OFF_MKV = OFF_RA + 2 * A_LORA
OFF_MKR = OFF_MKV + KV_LORA
IN_COLS = OFF_MKR + ROPE

N_EXPERTS = 16
N_GROUPS = 4
EXPERTS_PER_GROUP = N_EXPERTS // N_GROUPS
TOP_K = 2

LANES = 128
SUBLANES = 8
VMEM_LIMIT = 56 * 1024 * 1024
CONV_HALO = 16
MOE_ROWS = 512
CAST_BLOCK_ELEMS = 2 * 1024 * 1024
MM_ACC_ELEMS = 512 * 1024
MM_WEIGHT_BYTES = 28 * 1024 * 1024
GATHER_ROWS = 256
MLA_HEAD_GROUP = 4
ATTN_SUBTILE = 256

RC_R, RC_K, RC_V, RC_W, RC_A, RC_G = 0, C_R, 2 * C_R, 3 * C_R, 3 * C_R + 2 * W_LORA, 3 * C_R + 2 * W_LORA + 2 * A_LORA
G_PAD = 2 * LANES
RCOLS = RC_G + G_PAD
QH = NOPE + 2 * ROPE
KD = NOPE + ROPE

BF16 = jnp.bfloat16
F32 = jnp.float32
HP = lax.Precision.HIGHEST


def _params(sem):
    return pltpu.CompilerParams(dimension_semantics=sem, vmem_limit_bytes=VMEM_LIMIT)


def _tile(n, target, mult):
    best = None
    for d in range(mult, min(n, target) + 1, mult):
        if n % d == 0:
            best = d
    return n if best is None else best


def _silu(x):
    return x * jax.nn.sigmoid(x)


def _rms(x, g):
    return x * lax.rsqrt(jnp.mean(x * x, axis=-1, keepdims=True) + RMS_EPS) * g


def _cast_body(x_ref, o_ref):
    o_ref[...] = x_ref[...].astype(o_ref.dtype)


def cast_bf16(w, layer=None, col_off=0, cols=None):
    w3 = w[None] if layer is None else w
    lay = 0 if layer is None else layer
    lead = w3.shape[1:-1]
    w3 = w3.reshape(w3.shape[0], -1, w3.shape[-1])
    R = w3.shape[1]
    C = w3.shape[2] - col_off if cols is None else cols
    tc = _tile(math.gcd(C, col_off) if col_off else C, 4096, LANES)
    tr = _tile(R, max(CAST_BLOCK_ELEMS // tc, 2 * SUBLANES), 2 * SUBLANES)
    cb = col_off // tc
    out = pl.pallas_call(
        _cast_body, grid=(R // tr, C // tc),
        in_specs=[pl.BlockSpec((None, tr, tc), lambda i, j: (lay, i, j + cb))],
        out_specs=pl.BlockSpec((tr, tc), lambda i, j: (i, j)),
        out_shape=jax.ShapeDtypeStruct((R, C), BF16),
        compiler_params=_params(("parallel", "parallel")), name="cast_bf16")(w3)
    return out.reshape(lead + (C,))


def _mod_body(c_ref, w_ref, b_ref, o_ref):
    @pl.when(pl.program_id(0) == 0)
    def _():
        o_ref[...] = jnp.broadcast_to(b_ref[...], o_ref.shape)

    a = _silu(c_ref[...]).astype(BF16)
    o_ref[...] += jnp.dot(a, w_ref[...].astype(BF16), preferred_element_type=F32)


def adaln_mod(rows, w, b, layer):
    R, D = rows.shape
    N = w.shape[2]
    tk = LANES
    return pl.pallas_call(
        _mod_body,
        grid=(D // tk,),
        in_specs=[pl.BlockSpec((R, tk), lambda k: (0, k)),
                  pl.BlockSpec((None, tk, N), lambda k: (layer, k, 0)),
                  pl.BlockSpec((None, 1, N), lambda k: (layer, 0, 0))],
        out_specs=pl.BlockSpec((R, N), lambda k: (0, 0)),
        out_shape=jax.ShapeDtypeStruct((R, N), F32),
        compiler_params=_params(("arbitrary",)),
        name="adaln_mod",
    )(rows, w, b.reshape(b.shape[0], 1, N))


def _norm_mod_body(x_ref, g_ref, sc_ref, sh_ref, o_ref):
    o_ref[...] = (_rms(x_ref[...], g_ref[...]) * (1.0 + sc_ref[...]) + sh_ref[...]).astype(o_ref.dtype)


def _norm_mod_router_body(x_ref, g_ref, sc_ref, sh_ref, wr_ref, o_ref, lg_ref):
    y = _rms(x_ref[...], g_ref[...]) * (1.0 + sc_ref[...]) + sh_ref[...]
    o_ref[...] = y.astype(o_ref.dtype)
    lg_ref[...] = jnp.dot(y, wr_ref[...], precision=HP, preferred_element_type=F32)


def norm_mod(h, g, sc, sh, rows_per_mod, w_router=None):
    T, D = h.shape
    R = sc.shape[0]
    tm = _tile(math.gcd(T, rows_per_mod), 256, 2 * SUBLANES)
    mod_row = lambda i: (jnp.minimum(i * tm // rows_per_mod, R - 1), 0, 0)
    in_specs = [pl.BlockSpec((tm, D), lambda i: (i, 0)),
                pl.BlockSpec((1, D), lambda i: (0, 0)),
                pl.BlockSpec((None, 1, D), mod_row),
                pl.BlockSpec((None, 1, D), mod_row)]
    args = [h, g.reshape(1, D), sc.reshape(R, 1, D), sh.reshape(R, 1, D)]
    o_spec = pl.BlockSpec((tm, D), lambda i: (i, 0))
    o_shape = jax.ShapeDtypeStruct((T, D), BF16 if w_router is None else F32)
    if w_router is None:
        return pl.pallas_call(_norm_mod_body, grid=(T // tm,), in_specs=in_specs, out_specs=o_spec, out_shape=o_shape,
                              compiler_params=_params(("parallel",)), name="norm_mod")(*args)
    wr = jnp.pad(w_router, ((0, 0), (0, LANES - w_router.shape[1])))
    return pl.pallas_call(
        _norm_mod_router_body, grid=(T // tm,),
        in_specs=in_specs + [pl.BlockSpec((D, LANES), lambda i: (0, 0))],
        out_specs=[o_spec, pl.BlockSpec((tm, LANES), lambda i: (i, 0))],
        out_shape=[o_shape, jax.ShapeDtypeStruct((T, LANES), F32)],
        compiler_params=_params(("parallel",)), name="norm_mod_router")(*args, wr)


def _mm_body(*refs, n_extra, n_out, epi):
    a_ref, b_ref = refs[0], refs[1]
    extra = refs[2:2 + n_extra]
    outs = refs[2 + n_extra:2 + n_extra + n_out]
    acc_ref = refs[-1]
    k = pl.program_id(2)

    @pl.when(k == 0)
    def _():
        acc_ref[...] = jnp.zeros_like(acc_ref)

    acc_ref[...] += jnp.dot(a_ref[...], b_ref[...], preferred_element_type=F32)

    @pl.when(k == pl.num_programs(2) - 1)
    def _():
        epi(acc_ref[...], extra, outs)


def matmul(a, b, *, epi, out_shapes, out_specs, tm, tn, tk=None, extra=(), extra_specs=(),
           n_row_blocks=None, row_block_off=0, name="matmul"):
    M, K = a.shape
    N = b.shape[1]
    if tk is None:
        tk = K if 2 * K * tn * 2 <= MM_WEIGHT_BYTES else _tile(K, 1024, LANES)
    nrb = M // tm if n_row_blocks is None else n_row_blocks
    grid = (nrb, N // tn, K // tk)
    return pl.pallas_call(
        functools.partial(_mm_body, n_extra=len(extra), n_out=len(out_shapes), epi=epi),
        grid=grid,
        in_specs=[pl.BlockSpec((tm, tk), lambda i, j, k: (i + row_block_off, k)),
                  pl.BlockSpec((tk, tn), lambda i, j, k: (k, j))] + list(extra_specs),
        out_specs=list(out_specs), out_shape=list(out_shapes),
        scratch_shapes=[pltpu.VMEM((tm, tn), F32)],
        compiler_params=_params(("parallel", "parallel", "arbitrary")), name=name)(a, b, *extra)


def _epi_plain(acc, extra, outs):
    outs[0][...] = acc.astype(outs[0].dtype)


def matmul_plain(a, b, out_dtype=F32, name="matmul"):
    M, N = a.shape[0], b.shape[1]
    tn = _tile(N, 1024, 2 * LANES)
    tm = _tile(M, MM_ACC_ELEMS // tn, 2 * SUBLANES)
    return matmul(a, b, epi=_epi_plain, out_shapes=[jax.ShapeDtypeStruct((M, N), out_dtype)],
                  out_specs=[pl.BlockSpec((tm, tn), lambda i, j, k: (i, j))], tm=tm, tn=tn, name=name)[0]


def _rope_mix(t, tab):
    lane = lax.broadcasted_iota(jnp.int32, t.shape, 1)
    ss = jnp.sum(jnp.where(lane < ROPE, t * t, 0.0), axis=-1, keepdims=True)
    m = t * tab
    return (m + pltpu.roll(m, ROPE, axis=1)) * lax.rsqrt(ss * (1.0 / ROPE) + RMS_EPS)


def _epi_mq(acc, extra, outs):
    outs[0][...] = _rms(acc, extra[0][...]).astype(outs[0].dtype)


def _epi_mkv(acc, extra, outs):
    g_ref, tab_ref = extra
    outs[0][...] = _rms(acc[:, :KV_LORA], g_ref[...]).astype(outs[0].dtype)
    outs[1][...] = _rope_mix(acc[:, KV_LORA:], tab_ref[...])[:, :ROPE]


def _epi_qup(acc, extra, outs):
    g_ref, tab_ref = extra
    o = outs[0]
    for h in range(o.shape[0]):
        a = acc[:, h * QH:(h + 1) * QH]
        o[h, :, :NOPE] = (_rms(a[:, :NOPE], g_ref[...]) * Q_SCALE).astype(o.dtype)
        o[h, :, NOPE:] = (_rope_mix(a[:, NOPE:], tab_ref[...])[:, :ROPE] * Q_SCALE).astype(o.dtype)


def _epi_kvup(acc, extra, outs):
    g_ref, kr_ref = extra
    ok, ov = outs
    kr = kr_ref[...].astype(ok.dtype)
    for h in range(ok.shape[0]):
        a = acc[:, h * (NOPE + V_HEAD):(h + 1) * (NOPE + V_HEAD)]
        ok[h, :, :NOPE] = _rms(a[:, :NOPE], g_ref[...]).astype(ok.dtype)
        ok[h, :, NOPE:] = kr
        ov[h] = a[:, NOPE:].astype(ov.dtype)


def _epi_res(acc, extra, outs):
    res_ref, gate_ref = extra
    outs[0][...] = res_ref[...] + gate_ref[...] * acc


def mla_q_up(cq, w, qn_g, tab, B, Lx, row_off):
    tm = _tile(Lx, 512, 2 * SUBLANES)
    nt = Lx // tm
    hg = MLA_HEAD_GROUP
    return matmul(
        cq, w, epi=_epi_qup, tm=tm, tn=hg * QH, tk=Q_LORA,
        extra=(qn_g.reshape(1, NOPE), tab),
        extra_specs=(pl.BlockSpec((1, NOPE), lambda i, j, k: (0, 0)),
                     pl.BlockSpec((tm, 2 * ROPE), lambda i, j, k: (i % nt, 0))),
        out_shapes=[jax.ShapeDtypeStruct((B, N_HEADS_M, Lx, KD), BF16)],
        out_specs=[pl.BlockSpec((None, hg, tm, KD), lambda i, j, k: (i // nt, j, i % nt, 0))],
        n_row_blocks=B * nt, row_block_off=row_off // tm, name="mla_q_up")[0]


def mla_kv_up(ckv, w, kn_g, kr, B, Lx, row_off):
    tm = _tile(Lx, 512, 2 * SUBLANES)
    nt = Lx // tm
    rbo = row_off // tm
    hg = MLA_HEAD_GROUP
    return matmul(
        ckv, w, epi=_epi_kvup, tm=tm, tn=hg * (NOPE + V_HEAD), tk=KV_LORA,
        extra=(kn_g.reshape(1, NOPE), kr),
        extra_specs=(pl.BlockSpec((1, NOPE), lambda i, j, k: (0, 0)),
                     pl.BlockSpec((tm, ROPE), lambda i, j, k: (i + rbo, 0))),
        out_shapes=[jax.ShapeDtypeStruct((B, N_HEADS_M, Lx, KD), BF16),
                    jax.ShapeDtypeStruct((B, N_HEADS_M, Lx, V_HEAD), BF16)],
        out_specs=[pl.BlockSpec((None, hg, tm, KD), lambda i, j, k: (i // nt, j, i % nt, 0)),
                   pl.BlockSpec((None, hg, tm, V_HEAD), lambda i, j, k: (i // nt, j, i % nt, 0))],
        n_row_blocks=B * nt, row_block_off=rbo, name="mla_kv_up")


def _outproj_body(a0_ref, a1_ref, a2_ref, b_ref, res_ref, gate_ref, o_ref, acc_ref, *, bounds):
    k = pl.program_id(2)

    @pl.when(k == 0)
    def _():
        acc_ref[...] = jnp.zeros_like(acc_ref)

    for a_ref, (lo, hi) in zip((a0_ref, a1_ref, a2_ref), bounds):
        @pl.when((k >= lo) & (k < hi))
        def _():
            acc_ref[...] += jnp.dot(a_ref[...], b_ref[...], preferred_element_type=F32)

    @pl.when(k == pl.num_programs(2) - 1)
    def _():
        o_ref[...] = res_ref[...] + gate_ref[...] * acc_ref[...]


def out_proj(parts, w, res, gate, rows, rows_per_mod):
    N = w.shape[1]
    R = gate.shape[0]
    tm = _tile(math.gcd(rows, rows_per_mod), 1024, 2 * SUBLANES)
    tn = _tile(N, 1024, LANES)
    tk = _tile(math.gcd(*[p.shape[1] for p in parts]), 1024, LANES)
    bounds, lo = [], 0
    for p in parts:
        bounds.append((lo, lo + p.shape[1] // tk))
        lo = bounds[-1][1]

    def a_spec(b):
        return pl.BlockSpec((tm, tk), lambda i, j, k: (i, jnp.clip(k - b[0], 0, b[1] - b[0] - 1)))

    o_spec = pl.BlockSpec((tm, tn), lambda i, j, k: (i, j))
    return pl.pallas_call(
        functools.partial(_outproj_body, bounds=tuple(bounds)),
        grid=(rows // tm, N // tn, lo),
        in_specs=[a_spec(b) for b in bounds] + [
            pl.BlockSpec((tk, tn), lambda i, j, k: (k, j)), o_spec,
            pl.BlockSpec((None, 1, tn), lambda i, j, k: (jnp.minimum(i * tm // rows_per_mod, R - 1), 0, j))],
        out_specs=o_spec,
        out_shape=jax.ShapeDtypeStruct((rows, N), F32),
        scratch_shapes=[pltpu.VMEM((tm, tn), F32)],
        compiler_params=_params(("parallel", "parallel", "arbitrary")),
        name="out_proj")(*parts, w, res, gate.reshape(R, 1, N))


def _seq_row_block(b, i, t, L, Lc, nlat):
    nc = Lc // t
    return jnp.where(i < nc, (nlat + b * Lc) // t + i, (b * L) // t + (i - nc))


def _conv_body(u_ref, up_ref, un_ref, dw_ref, db_ref, g_ref, b_ref, o_ref, buf_ref, sh_ref, *, tl, rc, nc, i0):
    i = pl.program_id(1) + i0
    nt = pl.num_programs(1) + i0
    first_tile = (i == 0) | (i == nc)
    last_tile = (i == nc - 1) | (i == nt - 1)

    def glu(u):
        return u[:, :C_CONV] * jax.nn.sigmoid(u[:, C_CONV:])

    buf_ref[pl.ds(0, CONV_HALO), :] = jnp.where(first_tile, 0.0, glu(up_ref[...]))
    buf_ref[pl.ds(CONV_HALO, tl), :] = glu(u_ref[...])
    buf_ref[pl.ds(CONV_HALO + tl, CONV_HALO), :] = jnp.where(last_tile, 0.0, glu(un_ref[...]))
    nsh = tl + 2 * CONV_HALO - SUBLANES
    for s in range(1, SUBLANES):
        sh_ref[s - 1] = buf_ref[pl.ds(s, nsh), :]

    first = CONV_HALO - CONV_W // 2

    def chunk(c, carry):
        r0 = pl.multiple_of(c * rc, rc)
        acc = jnp.broadcast_to(db_ref[...], (rc, C_CONV))
        for k in range(CONV_W):
            q, s = divmod(first + k, SUBLANES)
            start = pl.multiple_of(r0 + q * SUBLANES, SUBLANES)
            win = buf_ref[pl.ds(start, rc), :] if s == 0 else sh_ref[s - 1, pl.ds(start, rc), :]
            acc = acc + dw_ref[pl.ds(k, 1), :] * win
        mu = jnp.mean(acc, axis=-1, keepdims=True)
        d = acc - mu
        var = jnp.mean(d * d, axis=-1, keepdims=True)
        y = d * lax.rsqrt(var + LN_EPS) * g_ref[...] + b_ref[...]
        o_ref[pl.ds(r0, rc), :] = _silu(y).astype(o_ref.dtype)
        return carry

    lax.fori_loop(0, tl // rc, chunk, 0)


def conformer_conv(u, dw, db, ln_g, ln_b, B, L, Lc, with_ctx):
    T = u.shape[0]
    nlat = B * L
    tl = _tile(math.gcd(L, Lc), 256, CONV_HALO)
    rc = _tile(tl, 32, SUBLANES)
    nc = Lc // tl
    i0 = 0 if with_ctx else nc
    hb = tl // CONV_HALO
    nhalo = T // CONV_HALO
    row = lambda b, i: _seq_row_block(b, i + i0, tl, L, Lc, nlat)
    body = functools.partial(_conv_body, tl=tl, rc=rc, nc=nc, i0=i0)
    vec = lambda a: a.reshape(1, C_CONV)
    vspec = pl.BlockSpec((1, C_CONV), lambda b, i: (0, 0))
    return pl.pallas_call(
        body,
        grid=(B, (L + Lc) // tl - i0),
        in_specs=[pl.BlockSpec((tl, 2 * C_CONV), lambda b, i: (row(b, i), 0)),
                  pl.BlockSpec((CONV_HALO, 2 * C_CONV), lambda b, i: (jnp.maximum(row(b, i) * hb - 1, 0), 0)),
                  pl.BlockSpec((CONV_HALO, 2 * C_CONV), lambda b, i: (jnp.minimum((row(b, i) + 1) * hb, nhalo - 1), 0)),
                  pl.BlockSpec((CONV_W, C_CONV), lambda b, i: (0, 0)),
                  vspec, vspec, vspec],
        out_specs=pl.BlockSpec((tl, C_CONV), lambda b, i: (row(b, i), 0)),
        out_shape=jax.ShapeDtypeStruct((T if with_ctx else nlat, C_CONV), BF16),
        scratch_shapes=[pltpu.VMEM((tl + 2 * CONV_HALO, C_CONV), F32),
                        pltpu.VMEM((SUBLANES - 1, tl + 2 * CONV_HALO - SUBLANES, C_CONV), F32)],
        compiler_params=_params(("parallel", "arbitrary")),
        name="conformer_conv",
    )(u, u, u, dw, vec(db), vec(ln_g), vec(ln_b))


def _softplus(z):
    return jnp.maximum(z, 0.0) + jnp.log(1.0 + jnp.exp(-jnp.abs(z)))


def _feat_body(pr_ref, prv_ref, nxt_ref, mup_ref, mun_ref, w0_ref, w2_ref, a0_ref, a2_ref, kkp_ref, ka_ref,
               rk_ref, g2_ref, bd_ref, p1f_ref, p1b_ref, p2f_ref, p2b_ref, p3_ref, bonus_ref, gate_ref, buf_ref,
               *, tm, nc):
    i = pl.program_id(1)
    nt = pl.num_programs(1)
    first_tile = (i == 0) | (i == nc)
    last_tile = (i == nc - 1) | (i == nt - 1)
    buf_ref[pl.ds(SUBLANES, tm), :] = pr_ref[...]
    buf_ref[pl.ds(0, SUBLANES), :] = jnp.where(first_tile, 0.0, prv_ref[...])
    buf_ref[pl.ds(SUBLANES + tm, SUBLANES), :] = jnp.where(last_tile, 0.0, nxt_ref[...])

    def shifted(lo, hi):
        p = buf_ref[pl.ds(SUBLANES, tm), lo:hi]
        pv = buf_ref[pl.ds(SUBLANES - 1, tm), lo:hi]
        nx = buf_ref[pl.ds(SUBLANES + 1, tm), lo:hi]
        return p + mup_ref[:, lo:hi] * (pv - p) + mun_ref[:, lo:hi] * (nx - p)

    def head_sum(x):
        return jnp.concatenate(
            [jnp.dot(x[:, c * LANES:(c + 1) * LANES], bd_ref[...], precision=HP, preferred_element_type=F32)
             for c in range(C_R // LANES)], axis=-1)

    r = shifted(RC_R, RC_K)
    k = shifted(RC_K, RC_V)
    v = shifted(RC_V, RC_W)
    wd = jnp.tanh(shifted(RC_W, RC_A))
    ad = shifted(RC_A, RC_G)
    rg = shifted(RC_G, RCOLS)
    wl = -_softplus(-(w0_ref[...] + jnp.dot(wd, w2_ref[...], precision=HP, preferred_element_type=F32))) - 0.5
    dec = jnp.exp(-jnp.exp(wl))
    a = jax.nn.sigmoid(a0_ref[...] + jnp.dot(ad, a2_ref[...], precision=HP, preferred_element_type=F32))
    kkv = k * kkp_ref[...]
    kk = kkv / jnp.maximum(jnp.sqrt(head_sum(kkv * kkv)), 1e-12)
    gate_ref[...] = jnp.dot(jax.nn.sigmoid(rg), g2_ref[...], precision=HP, preferred_element_type=F32)

    lane = lax.broadcasted_iota(jnp.int32, (tm, LANES), 1)
    low = lane < HEAD_R

    def pack(o_ref, x, y):
        heads = []
        for c in range(C_R // LANES):
            xa = x[:, c * LANES:(c + 1) * LANES]
            ya = y[:, c * LANES:(c + 1) * LANES]
            heads.append(jnp.where(low, xa, pltpu.roll(ya, HEAD_R, axis=1)))
            heads.append(jnp.where(low, pltpu.roll(xa, HEAD_R, axis=1), ya))
        o_ref[...] = pltpu.einshape("htl->thl", jnp.stack(heads, axis=0))

    bonus = jnp.zeros((tm, C_R), F32)
    for d, (o1, o2) in enumerate(((p1f_ref, p2f_ref), (p1b_ref, p2b_ref))):
        a_d = a[:, d * C_R:(d + 1) * C_R]
        k_d = k * (1.0 + (a_d - 1.0) * ka_ref[...])
        pack(o1, dec[:, d * C_R:(d + 1) * C_R], kk * a_d)
        pack(o2, k_d, kk)
        bonus = bonus + head_sum(r * k_d * rk_ref[...]) * v
    pack(p3_ref, v, r)
    bonus_ref[...] = bonus


def rwkv_features(pr, rp, B, L, Lc):
    T = pr.shape[0]
    nlat = B * L
    Lt = L + Lc
    tm = _tile(math.gcd(L, Lc), 256, SUBLANES)
    nc = Lc // tm
    nrow8 = T // SUBLANES
    hb = tm // SUBLANES
    row = lambda b, i: _seq_row_block(b, i, tm, L, Lc, nlat)
    full = lambda a: pl.BlockSpec(a.shape, lambda b, i: (0,) * a.ndim)
    consts = (rp["mu_prev"], rp["mu_next"], rp["w0"], rp["w2"], rp["a0"], rp["a2"], rp["kk"], rp["ka"], rp["rk"],
              rp["g2"], rp["bd"])
    pk_shape = jax.ShapeDtypeStruct((Lt, B * N_HEADS_R, 2 * HEAD_R), F32)
    pk_spec = pl.BlockSpec((tm, N_HEADS_R, 2 * HEAD_R), lambda b, i: (i, b, 0))
    tok_shape = jax.ShapeDtypeStruct((T, C_R), F32)
    tok_spec = pl.BlockSpec((tm, C_R), lambda b, i: (row(b, i), 0))
    return pl.pallas_call(
        functools.partial(_feat_body, tm=tm, nc=nc),
        grid=(B, Lt // tm),
        in_specs=[pl.BlockSpec((tm, RCOLS), lambda b, i: (row(b, i), 0)),
                  pl.BlockSpec((SUBLANES, RCOLS), lambda b, i: (jnp.maximum(row(b, i) * hb - 1, 0), 0)),
                  pl.BlockSpec((SUBLANES, RCOLS), lambda b, i: (jnp.minimum((row(b, i) + 1) * hb, nrow8 - 1), 0))]
                 + [full(a) for a in consts],
        out_specs=[pk_spec] * 5 + [tok_spec] * 2,
        out_shape=[pk_shape] * 5 + [tok_shape] * 2,
        scratch_shapes=[pltpu.VMEM((tm + 2 * SUBLANES, RCOLS), F32)],
        compiler_params=_params(("parallel", "arbitrary")),
        name="rwkv_features",
    )(pr, pr, pr, *consts)


def _fold8(parts):
    sub = lax.broadcasted_iota(jnp.int32, parts[0].shape, 0)
    d = 1
    while len(parts) > 1:
        m = (sub // d) % 2 == 0
        parts = [jnp.where(m, a, b) + pltpu.roll(jnp.where(m, b, a), d, axis=0)
                 for a, b in zip(parts[0::2], parts[1::2])]
        d *= 2
    return parts[0]


def _scan_body(p1f_ref, p1b_ref, p2f_ref, p2b_ref, p3f_ref, p3b_ref, yf_ref, yb_ref, s_ref, op_ref, ybuf_ref,
               *, tc, n, nch):
    @pl.when(pl.program_id(0) == 0)
    def _():
        s_ref[...] = jnp.zeros_like(s_ref)

    def prep(t, g_prev):
        tb = tc - 1 - t
        zt = [jnp.concatenate([f_ref[t], b_ref[tb]], axis=0).T
              for f_ref, b_ref in ((p1f_ref, p1b_ref), (p2f_ref, p2b_ref), (p3f_ref, p3b_ref))]
        g = g_prev * zt[0][:n]
        inv = 1.0 / g
        op_ref[t, 0] = zt[1][n:] * g_prev
        op_ref[t, 1] = zt[0][n:] * inv
        op_ref[t, 2] = zt[1][:n] * inv
        op_ref[t, 3] = zt[2][n:] * g
        op_ref[t, 4] = zt[2][:n]
        return g

    g_last = lax.fori_loop(0, tc, prep, jnp.ones((n, LANES), F32), unroll=4)
    nv = n // SUBLANES

    def colsum(x):
        return jnp.sum(x.reshape(nv, SUBLANES, x.shape[-1]), axis=0)

    def step(t, carry):
        kkg = op_ref[t, 0]
        kkai = op_ref[t, 1]
        ki = op_ref[t, 2]
        rg = op_ref[t, 3]

        def rows(ib, c2):
            base = pl.multiple_of(ib * SUBLANES, SUBLANES)
            sa8 = _fold8([colsum(s_ref[base + ii] * kkg) for ii in range(SUBLANES)])
            v8 = op_ref[t, 4, pl.ds(base, SUBLANES), :]
            py = []
            for ii in range(SUBLANES):
                s_new = s_ref[base + ii] - sa8[ii:ii + 1] * kkai + v8[ii:ii + 1] * ki
                s_ref[base + ii] = s_new
                py.append(colsum(s_new * rg))
            ybuf_ref[t, pl.ds(base, SUBLANES), :] = _fold8(py)
            return c2

        lax.fori_loop(0, nv, rows, 0, unroll=True)
        return carry

    lax.fori_loop(0, tc, step, 0)

    def rescale(i, carry):
        s_ref[i] = s_ref[i] * g_last
        return carry

    lax.fori_loop(0, n, rescale, 0, unroll=4)

    def fin(t, carry):
        y = ybuf_ref[t]
        yt = jnp.concatenate([y, y], axis=0).T
        yf_ref[t] = yt[:nch, :n]
        yb_ref[tc - 1 - t] = yt[nch:, :n]
        return carry

    lax.fori_loop(0, tc, fin, 0, unroll=4)


def wkv_scan(p1f, p1b, p2f, p2b, p3, Lc):
    Lt, nch, _ = p1f.shape
    n = HEAD_R
    assert 2 * nch == LANES and 2 * n == LANES
    tc = _tile(math.gcd(Lc, Lt - Lc), 32, 1)
    ncb, nb = Lc // tc, Lt // tc
    fwd = lambda g: (g, 0, 0)
    bwd = lambda g: (jnp.where(g < ncb, ncb - 1 - g, nb - 1 - (g - ncb)), 0, 0)
    ispec = lambda m: pl.BlockSpec((tc, nch, 2 * n), m)
    ospec = lambda m: pl.BlockSpec((tc, nch, n), m)
    y_shape = jax.ShapeDtypeStruct((Lt, nch, n), F32)
    return pl.pallas_call(
        functools.partial(_scan_body, tc=tc, n=n, nch=nch),
        grid=(nb,),
        in_specs=[ispec(fwd), ispec(bwd), ispec(fwd), ispec(bwd), ispec(fwd), ispec(bwd)],
        out_specs=[ospec(fwd), ospec(bwd)],
        out_shape=[y_shape, y_shape],
        scratch_shapes=[pltpu.VMEM((n, n, LANES), F32), pltpu.VMEM((tc, 5, n, LANES), F32),
                        pltpu.VMEM((tc, n, LANES), F32)],
        compiler_params=_params(("arbitrary",)),
        name="wkv_scan",
    )(p1f, p1b, p2f, p2b, p3, p3)


def _post_body(yf_ref, yb_ref, bonus_ref, gate_ref, g_ref, b_ref, o_ref):
    yh = pltpu.einshape("thl->htl", yf_ref[...] + yb_ref[...])
    cols = []
    for h in range(N_HEADS_R):
        y = yh[h]
        mu = jnp.mean(y, axis=-1, keepdims=True)
        d = y - mu
        var = jnp.mean(d * d, axis=-1, keepdims=True)
        cols.append(d * lax.rsqrt(var + GN_EPS))
    yn = jnp.concatenate(cols, axis=-1) * g_ref[...] + b_ref[...]
    o_ref[...] = ((yn + bonus_ref[...]) * gate_ref[...]).astype(o_ref.dtype)


def rwkv_output(yf, yb, bonus, gate, ln_g, ln_b, B, L, Lc, with_ctx):
    T = bonus.shape[0]
    nlat = B * L
    tm = _tile(math.gcd(L, Lc), 256, 2 * SUBLANES)
    i0 = 0 if with_ctx else Lc // tm
    row = lambda b, i: _seq_row_block(b, i + i0, tm, L, Lc, nlat)
    y_spec = pl.BlockSpec((tm, N_HEADS_R, HEAD_R), lambda b, i: (i + i0, b, 0))
    tok_spec = pl.BlockSpec((tm, C_R), lambda b, i: (row(b, i), 0))
    vspec = pl.BlockSpec((1, C_R), lambda b, i: (0, 0))
    return pl.pallas_call(
        _post_body,
        grid=(B, (L + Lc) // tm - i0),
        in_specs=[y_spec, y_spec, tok_spec, tok_spec, vspec, vspec],
        out_specs=tok_spec,
        out_shape=jax.ShapeDtypeStruct((T if with_ctx else nlat, C_R), BF16),
        compiler_params=_params(("parallel", "parallel")),
        name="rwkv_output",
    )(yf, yb, bonus, gate, ln_g.reshape(1, C_R), ln_b.reshape(1, C_R))


def _attn_body(*refs, n_kv, hq):
    kv = refs[1:1 + 2 * n_kv]
    o_ref = refs[-1]
    for a in range(refs[0].shape[0] // hq):
        q = refs[0][pl.ds(a * hq, hq), :]
        s = [lax.dot_general(q, kv[2 * j][...], (((1,), (1,)), ((), ())), preferred_element_type=F32)
             for j in range(n_kv)]
        m = functools.reduce(jnp.maximum, [jnp.max(x, axis=-1, keepdims=True) for x in s])
        p = [jnp.exp2(x - m) for x in s]
        l = functools.reduce(jnp.add, [jnp.sum(x, axis=-1, keepdims=True) for x in p])
        o = functools.reduce(jnp.add, [jnp.dot(p[j].astype(BF16), kv[2 * j + 1][...], preferred_element_type=F32)
                                       for j in range(n_kv)])
        o_ref[pl.ds(a * hq, hq), :] = (o / l).astype(o_ref.dtype)


def mla_attention(q, kvs, rows_total, row_off, prev=None):
    B, H, Lq, dk = q.shape
    dv = kvs[0][1].shape[3]
    hq = _tile(Lq, ATTN_SUBTILE, 2 * SUBLANES)
    tq = _tile(Lq, 2 * hq, hq)
    nq = Lq // tq
    rbo = row_off // tq
    in_specs = [pl.BlockSpec((None, None, tq, dk), lambda b, h, i: (b, h, i, 0))]
    args = [q]
    for k, v in kvs:
        in_specs += [pl.BlockSpec((None, None, k.shape[2], dk), lambda b, h, i: (b, h, 0, 0)),
                     pl.BlockSpec((None, None, v.shape[2], dv), lambda b, h, i: (b, h, 0, 0))]
        args += [k, v]
    body = functools.partial(_attn_body, n_kv=len(kvs), hq=hq)
    aliases = {}
    if prev is not None:
        in_specs.append(pl.BlockSpec(memory_space=pl.ANY))
        args.append(prev)
        aliases = {len(args) - 1: 0}
        body = lambda *refs: _attn_body(*refs[:-2], refs[-1], n_kv=len(kvs), hq=hq)
    return pl.pallas_call(
        body,
        grid=(B, H, nq),
        in_specs=in_specs,
        out_specs=pl.BlockSpec((tq, dv), lambda b, h, i: (rbo + b * nq + i, h)),
        out_shape=jax.ShapeDtypeStruct((rows_total, H * dv), BF16),
        input_output_aliases=aliases,
        compiler_params=_params(("parallel", "parallel", "arbitrary")),
        name="mla_attention",
    )(*args)


def _moe_body(be_ref, nu_ref, x_ref, w1_ref, w3_ref, w2_ref, o_ref):
    blk = pl.program_id(0)
    f = pl.program_id(1)

    @pl.when(f == 0)
    def _():
        o_ref[...] = jnp.zeros_like(o_ref)

    @pl.when(blk < nu_ref[0])
    def _():
        x = x_ref[...]
        h1 = jnp.dot(x, w1_ref[...], preferred_element_type=F32)
        h3 = jnp.dot(x, w3_ref[...], preferred_element_type=F32)
        hb = (_silu(h1) * h3).astype(BF16)
        o_ref[...] += jnp.dot(hb, w2_ref[...], preferred_element_type=F32)


def moe_experts(xs, block_e, n_used, w1, w3, w2):
    P, D = xs.shape
    Fd = w1.shape[2]
    tf = _tile(Fd, 256, LANES)
    nf = Fd // tf
    nb = P // MOE_ROWS

    def fidx(i, f, nu):
        return jnp.where(i < nu[0], f, nf - 1)

    grid_spec = pltpu.PrefetchScalarGridSpec(
        num_scalar_prefetch=2,
        grid=(nb, nf),
        in_specs=[pl.BlockSpec((MOE_ROWS, D), lambda i, f, be, nu: (i, 0)),
                  pl.BlockSpec((None, D, tf), lambda i, f, be, nu: (be[i], 0, fidx(i, f, nu))),
                  pl.BlockSpec((None, D, tf), lambda i, f, be, nu: (be[i], 0, fidx(i, f, nu))),
                  pl.BlockSpec((None, tf, D), lambda i, f, be, nu: (be[i], fidx(i, f, nu), 0))],
        out_specs=pl.BlockSpec((MOE_ROWS, D), lambda i, f, be, nu: (i, 0)),
    )
    return pl.pallas_call(
        _moe_body,
        grid_spec=grid_spec,
        out_shape=jax.ShapeDtypeStruct((P, D), F32),
        compiler_params=_params(("arbitrary", "arbitrary")),
        name="moe_experts",
    )(block_e, n_used, xs, w1, w3, w2)


def _gather_body(*refs, n_src, tm, combine):
    idx_refs = refs[:n_src]
    src_ref = refs[n_src]
    rest = refs[n_src + 1:]
    if combine:
        w_ref, base_ref, gate_ref, o_ref, buf_ref, sem_ref = rest
    else:
        o_ref, buf_ref, sem_ref = rest
    i = pl.program_id(0)
    nsteps = pl.num_programs(0)
    slot = i % 2

    def row_copy(step, r, j, sl):
        row = idx_refs[j][step * tm + r]
        return pltpu.make_async_copy(src_ref.at[pl.ds(row, 1), :], buf_ref.at[sl, j, pl.ds(r, 1), :], sem_ref.at[sl])

    def start_all(step, sl):
        def body(r, c):
            for j in range(n_src):
                row_copy(step, r, j, sl).start()
            return c
        lax.fori_loop(0, tm, body, 0, unroll=8)

    def wait_all(step, sl):
        def body(r, c):
            for j in range(n_src):
                row_copy(step, r, j, sl).wait()
            return c
        lax.fori_loop(0, tm, body, 0, unroll=8)

    @pl.when(i == 0)
    def _():
        start_all(0, 0)

    @pl.when(i + 1 < nsteps)
    def _():
        start_all(i + 1, 1 - slot)

    wait_all(i, slot)
    if combine:
        acc = buf_ref[slot, 0] * w_ref[:, 0:1]
        for j in range(1, n_src):
            acc = acc + buf_ref[slot, j] * w_ref[:, j:j + 1]
        o_ref[...] = base_ref[...] + gate_ref[...] * acc
    else:
        o_ref[...] = buf_ref[slot, 0].astype(o_ref.dtype)


def gather_rows(src, idx, out_dtype):
    R = idx.shape[0]
    D = src.shape[1]
    tm = _tile(R, GATHER_ROWS, 2 * SUBLANES)
    grid_spec = pltpu.PrefetchScalarGridSpec(
        num_scalar_prefetch=1, grid=(R // tm,),
        in_specs=[pl.BlockSpec(memory_space=pl.ANY)],
        out_specs=pl.BlockSpec((tm, D), lambda i, ix: (i, 0)),
        scratch_shapes=[pltpu.VMEM((2, 1, tm, D), F32), pltpu.SemaphoreType.DMA((2,))])
    return pl.pallas_call(
        functools.partial(_gather_body, n_src=1, tm=tm, combine=False), grid_spec=grid_spec,
        out_shape=jax.ShapeDtypeStruct((R, D), out_dtype),
        compiler_params=_params(("arbitrary",)), name="moe_dispatch")(idx, src)


def combine_rows(src, idxs, w, base, gate, rows_per_mod):
    T, D = base.shape
    n_src = len(idxs)
    Rg = gate.shape[0]
    tm = _tile(math.gcd(T, rows_per_mod), GATHER_ROWS // n_src, SUBLANES)
    grid_spec = pltpu.PrefetchScalarGridSpec(
        num_scalar_prefetch=n_src, grid=(T // tm,),
        in_specs=[pl.BlockSpec(memory_space=pl.ANY),
                  pl.BlockSpec((tm, n_src), lambda i, *ix: (i, 0)),
                  pl.BlockSpec((tm, D), lambda i, *ix: (i, 0)),
                  pl.BlockSpec((None, 1, D), lambda i, *ix: (jnp.minimum(i * tm // rows_per_mod, Rg - 1), 0, 0))],
        out_specs=pl.BlockSpec((tm, D), lambda i, *ix: (i, 0)),
        scratch_shapes=[pltpu.VMEM((2, n_src, tm, D), F32), pltpu.SemaphoreType.DMA((2,))])
    return pl.pallas_call(
        functools.partial(_gather_body, n_src=n_src, tm=tm, combine=True), grid_spec=grid_spec,
        out_shape=jax.ShapeDtypeStruct((T, D), F32),
        compiler_params=_params(("arbitrary",)), name="moe_combine")(*idxs, src, w, base, gate.reshape(Rg, 1, D))


def _top2(x):
    pos = lax.broadcasted_iota(jnp.int32, x.shape, x.ndim - 1)
    i1 = jnp.argmax(x, -1)
    m1 = jnp.max(x, -1)
    x2 = jnp.where(pos == i1[..., None], -jnp.inf, x)
    return jnp.stack([m1, jnp.max(x2, -1)], -1), jnp.stack([i1, jnp.argmax(x2, -1)], -1)


def _route(logits, router_bias):
    scores = jax.nn.sigmoid(logits)
    biased = (scores + router_bias.astype(F32)).reshape(-1, N_GROUPS, EXPERTS_PER_GROUP)
    group_score = jnp.sum(_top2(biased)[0], -1)
    chosen = jnp.argmax(group_score, -1)
    in_group = jnp.arange(N_GROUPS)[None, :] == chosen[:, None]
    masked = jnp.where(in_group[:, :, None], biased, -jnp.inf).reshape(-1, N_EXPERTS)
    _, idx = _top2(masked)
    w = jnp.take_along_axis(scores, idx, -1)
    return idx, w / jnp.sum(w, -1, keepdims=True)


def moe_ffn(tok, logits, router_bias, w1, w3, w2, base, gate, rows_per_mod):
    T, D = tok.shape
    idx, wts = _route(logits, router_bias)
    M = T * TOP_K
    flat_e = idx.reshape(-1).astype(jnp.int32)
    onehot = (flat_e[:, None] == jnp.arange(N_EXPERTS, dtype=jnp.int32)[None, :]).astype(jnp.int32)
    csum = jnp.cumsum(onehot, axis=0)
    rank = jnp.take_along_axis(csum, flat_e[:, None], axis=1)[:, 0] - 1
    counts = csum[-1]
    pcounts = (counts + MOE_ROWS - 1) // MOE_ROWS * MOE_ROWS
    pends = jnp.cumsum(pcounts)
    pstarts = pends - pcounts
    dest = pstarts[flat_e] + rank
    n_blocks = -(-M // MOE_ROWS) + N_EXPERTS
    P = n_blocks * MOE_ROWS
    flat_t = jnp.repeat(jnp.arange(T, dtype=jnp.int32), TOP_K)
    row_tok = jnp.zeros((P,), jnp.int32).at[dest].set(flat_t)
    n_used = (pends[-1] // MOE_ROWS).astype(jnp.int32)
    blk_start = jnp.arange(n_blocks, dtype=jnp.int32) * MOE_ROWS
    block_e = jnp.clip(jnp.searchsorted(pends, blk_start, side="right"), 0, N_EXPERTS - 1).astype(jnp.int32)
    last_e = block_e[jnp.maximum(n_used - 1, 0)]
    block_e = jnp.where(jnp.arange(n_blocks) < n_used, block_e, last_e)
    xs = gather_rows(tok, row_tok, BF16)
    yb = moe_experts(xs, block_e, n_used.reshape(1), w1, w3, w2)
    dest = dest.reshape(T, TOP_K)
    return combine_rows(yb, [dest[:, j] for j in range(TOP_K)], wts, base, gate, rows_per_mod)


def _rope_tables(rows):
    half = ROPE // 2
    inv_freq = ROPE_BASE ** (-jnp.arange(0, half, 2, dtype=F32) / half)
    r = jnp.repeat(jnp.arange(rows, dtype=F32), GRID_W)
    cl = jnp.tile(jnp.arange(GRID_W, dtype=F32), rows)
    ar = r[:, None] * inv_freq
    ac = cl[:, None] * inv_freq
    ang = jnp.concatenate([ar, ar, ac, ac], -1)
    return jnp.cos(ang), jnp.sin(ang)


def _rot_cols(w):
    q = ROPE // 4
    return jnp.concatenate([-w[..., q:2 * q], w[..., :q], -w[..., 3 * q:], w[..., 2 * q:3 * q]], -1)


def _rot_perm(g):
    q = ROPE // 4
    return jnp.concatenate([g[q:2 * q], g[:q], g[3 * q:], g[2 * q:3 * q]], -1)


def _rope_tab(g, cos, sin):
    if cos is None:
        return jnp.concatenate([g, jnp.zeros_like(g)])[None, :]
    return jnp.concatenate([g[None, :] * cos, _rot_perm(g)[None, :] * sin], -1)


def _blockdiag2(w):
    z = jnp.zeros_like(w[0])
    return jnp.concatenate([jnp.concatenate([w[0], z], 1), jnp.concatenate([z, w[1]], 1)], 0)


def kernel(x, c, ctx, c_ctx, w_mod, b_mod, norm1_g, norm2_g, w_in, w_out, conv_dw, conv_b, conv_ln_g, conv_ln_b, r_mu, r_w0, r_w2, r_a0, r_a2, r_g2, r_kk, r_ka, r_rk, r_ln_g, r_ln_b, m_cq_g, m_w_uq, m_ckv_g, m_w_ukv, m_qn_g, m_qr_g, m_kn_g, m_kr_g, w_router, router_bias, moe_w1, moe_w3, moe_w2):
    B, L, D = x.shape
    Lc = ctx.shape[1]
    depth = w_mod.shape[0]
    nlat, nctx = B * L, B * Lc
    T = nlat + nctx
    cos, sin = _rope_tables(L // GRID_W)
    tm_in = _tile(math.gcd(L, nctx), 512, 2 * SUBLANES)
    bd = jnp.kron(jnp.eye(LANES // HEAD_R, dtype=F32), jnp.ones((HEAD_R, HEAD_R), F32))

    H = jnp.concatenate([x.reshape(nlat, D), ctx.reshape(nctx, D)], 0)
    crow = jnp.concatenate([c, c_ctx[None, :], jnp.zeros((SUBLANES - (B + 1) % SUBLANES, D), F32)], 0)

    for l in range(depth):
        last = l == depth - 1
        mod = adaln_mod(crow, w_mod, b_mod, l)[:B + 1]
        sh1, sc1, ga1, sh2, sc2, ga2 = jnp.split(mod, 6, axis=-1)
        n = norm_mod(H, norm1_g[l], sc1, sh1, L)

        wi = w_in[l]
        cols = lambda lo, hi: wi[:, lo:hi]
        w_conv = cast_bf16(w_in, l, OFF_CONV, OFF_MQ - OFF_CONV)
        w_mq = cast_bf16(w_in, l, OFF_MQ, OFF_RR - OFF_MQ)
        w_rw = cast_bf16(jnp.concatenate(
            [cols(OFF_RR, OFF_RG), cols(OFF_RK, OFF_RV), cols(OFF_RV, OFF_RW), cols(OFF_RW, OFF_RA),
             cols(OFF_RA, OFF_MKV), cols(OFF_RG, OFF_RK), jnp.zeros((D, G_PAD - G_LORA), F32)], 1))
        w_kv = cast_bf16(jnp.concatenate(
            [cols(OFF_MKV, OFF_MKR), cols(OFF_MKR, IN_COLS), _rot_cols(cols(OFF_MKR, IN_COLS))], 1))
        u = matmul_plain(n, w_conv, name="in_conv")
        pr = matmul_plain(n, w_rw, name="in_rwkv")
        cq = matmul(n, w_mq, epi=_epi_mq, tm=tm_in, tn=Q_LORA,
                    extra=(m_cq_g[l].reshape(1, Q_LORA),),
                    extra_specs=(pl.BlockSpec((1, Q_LORA), lambda i, j, k: (0, 0)),),
                    out_shapes=[jax.ShapeDtypeStruct((T, Q_LORA), BF16)],
                    out_specs=[pl.BlockSpec((tm_in, Q_LORA), lambda i, j, k: (i, 0))], name="in_mq")[0]
        ktab = jnp.concatenate([_rope_tab(m_kr_g[l], cos, sin),
                                jnp.broadcast_to(_rope_tab(m_kr_g[l], None, None), (tm_in, 2 * ROPE))], 0)
        nlb = L // tm_in
        ckv, kr = matmul(n, w_kv, epi=_epi_mkv, tm=tm_in, tn=KV_LORA + 2 * ROPE,
                         extra=(m_ckv_g[l].reshape(1, KV_LORA), ktab),
                         extra_specs=(pl.BlockSpec((1, KV_LORA), lambda i, j, k: (0, 0)),
                                      pl.BlockSpec((tm_in, 2 * ROPE),
                                                   lambda i, j, k: (jnp.where(i < B * nlb, i % nlb, nlb), 0))),
                         out_shapes=[jax.ShapeDtypeStruct((T, KV_LORA), BF16), jax.ShapeDtypeStruct((T, ROPE), F32)],
                         out_specs=[pl.BlockSpec((tm_in, KV_LORA), lambda i, j, k: (i, 0)),
                                    pl.BlockSpec((tm_in, ROPE), lambda i, j, k: (i, 0))], name="in_mkv")

        conv = conformer_conv(u, conv_dw[l], conv_b[l], conv_ln_g[l], conv_ln_b[l], B, L, Lc, not last)

        mu = r_mu[l]
        mcols = lambda lo, hi: mu[:, lo - OFF_RR:hi - OFF_RR]
        mu_r = jnp.concatenate([mcols(OFF_RR, OFF_RG), mcols(OFF_RK, OFF_RV), mcols(OFF_RV, OFF_RW),
                                mcols(OFF_RW, OFF_RA), mcols(OFF_RA, OFF_MKV), mcols(OFF_RG, OFF_RK),
                                jnp.zeros((2, G_PAD - G_LORA), F32)], 1)
        rp = {"mu_prev": mu_r[0:1], "mu_next": mu_r[1:2],
              "w0": r_w0[l].reshape(1, 2 * C_R), "w2": _blockdiag2(r_w2[l]),
              "a0": r_a0[l].reshape(1, 2 * C_R), "a2": _blockdiag2(r_a2[l]),
              "kk": r_kk[l].reshape(1, C_R), "ka": r_ka[l].reshape(1, C_R), "rk": r_rk[l].reshape(1, C_R),
              "g2": jnp.pad(r_g2[l], ((0, G_PAD - G_LORA), (0, 0))), "bd": bd}
        p1f, p1b, p2f, p2b, p3, bonus, gate = rwkv_features(pr, rp, B, L, Lc)
        yf, yb = wkv_scan(p1f, p1b, p2f, p2b, p3, Lc)
        rw = rwkv_output(yf, yb, bonus, gate, r_ln_g[l], r_ln_b[l], B, L, Lc, not last)

        wq = m_w_uq[l].reshape(Q_LORA, N_HEADS_M, NOPE + ROPE)
        wq = cast_bf16(jnp.concatenate([wq, _rot_cols(wq[..., NOPE:])], -1).reshape(Q_LORA, N_HEADS_M * QH))
        wkv = cast_bf16(m_w_ukv, l)
        k_c, v_c = mla_kv_up(ckv, wkv, m_kn_g[l], kr, B, Lc, nlat)
        k_l, v_l = mla_kv_up(ckv, wkv, m_kn_g[l], kr, B, L, 0)
        q_l = mla_q_up(cq, wq, m_qn_g[l], _rope_tab(m_qr_g[l], cos, sin), B, L, 0)
        rows_out = nlat if last else T
        mla = mla_attention(q_l, [(k_c, v_c), (k_l, v_l)], rows_out, 0,
                            prev=None if last else jnp.zeros((T, C_M), BF16))
        if not last:
            tq_c = _tile(Lc, 512, 2 * SUBLANES)
            q_c = mla_q_up(cq, wq, m_qn_g[l], jnp.broadcast_to(_rope_tab(m_qr_g[l], None, None), (tq_c, 2 * ROPE)),
                           B, Lc, nlat)
            mla = mla_attention(q_c, [(k_c, v_c)], rows_out, nlat, prev=mla)

        Hn = out_proj((conv, rw, mla), cast_bf16(w_out, l), H, ga1, rows_out, L)
        n2, logits = norm_mod(Hn, norm2_g[l], sc2, sh2, L, w_router=w_router)
        H = moe_ffn(n2, logits[:, :N_EXPERTS], router_bias,
                    cast_bf16(moe_w1, l), cast_bf16(moe_w3, l), cast_bf16(moe_w2, l), Hn, ga2, L)
    return H[:nlat].reshape(B, L, D)
```

```python
import functools
import math

import jax
import jax.numpy as jnp
from jax import lax
from jax.experimental import pallas as pl
from jax.experimental.pallas import tpu as pltpu

GRID_W = 64
C_CONV = 1024
CONV_W = 31
N_HEADS_R = 16
HEAD_R = 64
C_R = N_HEADS_R * HEAD_R
W_LORA = 64
A_LORA = 64
G_LORA = 160
GN_EPS = 64e-5
N_HEADS_M = 16
NOPE = 128
ROPE = 64
V_HEAD = 128
Q_LORA = 1536
KV_LORA = 512
C_M = N_HEADS_M * V_HEAD
ROPE_BASE = 10000.0
ATTN_SCALE = (NOPE + ROPE) ** -0.5
Q_SCALE = ATTN_SCALE * math.log2(math.e)
RMS_EPS = 1e-6
LN_EPS = 1e-5

OFF_CONV = 0
OFF_MQ = OFF_CONV + 2 * C_CONV
OFF_RR = OFF_MQ + Q_LORA
OFF_RG = OFF_RR + C_R
OFF_RK = OFF_RG + G_LORA
OFF_RV = OFF_RK + C_R
OFF_RW = OFF_RV + C_R
OFF_RA = OFF_RW + 2 * W_LORA
OFF_MKV = OFF_RA + 2 * A_LORA
OFF_MKR = OFF_MKV + KV_LORA
IN_COLS = OFF_MKR + ROPE

N_EXPERTS = 16
N_GROUPS = 4
EXPERTS_PER_GROUP = N_EXPERTS // N_GROUPS
TOP_K = 2

LANES = 128
SUBLANES = 8
VMEM_LIMIT = 56 * 1024 * 1024
CONV_HALO = 16
MOE_ROWS = 512
CAST_BLOCK_ELEMS = 2 * 1024 * 1024
MM_ACC_ELEMS = 512 * 1024
MM_WEIGHT_BYTES = 28 * 1024 * 1024
GATHER_ROWS = 256
MLA_HEAD_GROUP = 4
ATTN_SUBTILE = 256

RC_R, RC_K, RC_V, RC_W, RC_A, RC_G = 0, C_R, 2 * C_R, 3 * C_R, 3 * C_R + 2 * W_LORA, 3 * C_R + 2 * W_LORA + 2 * A_LORA
G_PAD = 2 * LANES
RCOLS = RC_G + G_PAD
QH = NOPE + 2 * ROPE
KD = NOPE + ROPE

BF16 = jnp.bfloat16
F32 = jnp.float32
HP = lax.Precision.HIGHEST


def _params(sem):
    return pltpu.CompilerParams(dimension_semantics=sem, vmem_limit_bytes=VMEM_LIMIT)


def _tile(n, target, mult):
    best = None
    for d in range(mult, min(n, target) + 1, mult):
        if n % d == 0:
            best = d
    return n if best is None else best


def _silu(x):
    return x * jax.nn.sigmoid(x)


def _rms(x, g):
    return x * lax.rsqrt(jnp.mean(x * x, axis=-1, keepdims=True) + RMS_EPS) * g


def _cast_body(x_ref, o_ref):
    o_ref[...] = x_ref[...].astype(o_ref.dtype)


def cast_bf16(w, layer=None, col_off=0, cols=None):
    w3 = w[None] if layer is None else w
    lay = 0 if layer is None else layer
    lead = w3.shape[1:-1]
    w3 = w3.reshape(w3.shape[0], -1, w3.shape[-1])
    R = w3.shape[1]
    C = w3.shape[2] - col_off if cols is None else cols
    tc = _tile(math.gcd(C, col_off) if col_off else C, 4096, LANES)
    tr = _tile(R, max(CAST_BLOCK_ELEMS // tc, 2 * SUBLANES), 2 * SUBLANES)
    cb = col_off // tc
    out = pl.pallas_call(
        _cast_body, grid=(R // tr, C // tc),
        in_specs=[pl.BlockSpec((None, tr, tc), lambda i, j: (lay, i, j + cb))],
        out_specs=pl.BlockSpec((tr, tc), lambda i, j: (i, j)),
        out_shape=jax.ShapeDtypeStruct((R, C), BF16),
        compiler_params=_params(("parallel", "parallel")), name="cast_bf16")(w3)
    return out.reshape(lead + (C,))


def _mod_body(c_ref, w_ref, b_ref, o_ref):
    @pl.when(pl.program_id(0) == 0)
    def _():
        o_ref[...] = jnp.broadcast_to(b_ref[...], o_ref.shape)

    a = _silu(c_ref[...]).astype(BF16)
    o_ref[...] += jnp.dot(a, w_ref[...].astype(BF16), preferred_element_type=F32)


def adaln_mod(rows, w, b, layer):
    R, D = rows.shape
    N = w.shape[2]
    tk = LANES
    return pl.pallas_call(
        _mod_body,
        grid=(D // tk,),
        in_specs=[pl.BlockSpec((R, tk), lambda k: (0, k)),
                  pl.BlockSpec((None, tk, N), lambda k: (layer, k, 0)),
                  pl.BlockSpec((None, 1, N), lambda k: (layer, 0, 0))],
        out_specs=pl.BlockSpec((R, N), lambda k: (0, 0)),
        out_shape=jax.ShapeDtypeStruct((R, N), F32),
        compiler_params=_params(("arbitrary",)),
        name="adaln_mod",
    )(rows, w, b.reshape(b.shape[0], 1, N))


def _norm_mod_body(x_ref, g_ref, sc_ref, sh_ref, o_ref):
    o_ref[...] = (_rms(x_ref[...], g_ref[...]) * (1.0 + sc_ref[...]) + sh_ref[...]).astype(o_ref.dtype)


def _norm_mod_router_body(x_ref, g_ref, sc_ref, sh_ref, wr_ref, o_ref, lg_ref):
    y = _rms(x_ref[...], g_ref[...]) * (1.0 + sc_ref[...]) + sh_ref[...]
    o_ref[...] = y.astype(o_ref.dtype)
    lg_ref[...] = jnp.dot(y, wr_ref[...], precision=HP, preferred_element_type=F32)


def norm_mod(h, g, sc, sh, rows_per_mod, w_router=None):
    T, D = h.shape
    R = sc.shape[0]
    tm = _tile(math.gcd(T, rows_per_mod), 256, 2 * SUBLANES)
    mod_row = lambda i: (jnp.minimum(i * tm // rows_per_mod, R - 1), 0, 0)
    in_specs = [pl.BlockSpec((tm, D), lambda i: (i, 0)),
                pl.BlockSpec((1, D), lambda i: (0, 0)),
                pl.BlockSpec((None, 1, D), mod_row),
                pl.BlockSpec((None, 1, D), mod_row)]
    args = [h, g.reshape(1, D), sc.reshape(R, 1, D), sh.reshape(R, 1, D)]
    o_spec = pl.BlockSpec((tm, D), lambda i: (i, 0))
    o_shape = jax.ShapeDtypeStruct((T, D), BF16 if w_router is None else F32)
    if w_router is None:
        return pl.pallas_call(_norm_mod_body, grid=(T // tm,), in_specs=in_specs, out_specs=o_spec, out_shape=o_shape,
                              compiler_params=_params(("parallel",)), name="norm_mod")(*args)
    wr = jnp.pad(w_router, ((0, 0), (0, LANES - w_router.shape[1])))
    return pl.pallas_call(
        _norm_mod_router_body, grid=(T // tm,),
        in_specs=in_specs + [pl.BlockSpec((D, LANES), lambda i: (0, 0))],
        out_specs=[o_spec, pl.BlockSpec((tm, LANES), lambda i: (i, 0))],
        out_shape=[o_shape, jax.ShapeDtypeStruct((T, LANES), F32)],
        compiler_params=_params(("parallel",)), name="norm_mod_router")(*args, wr)


def _mm_body(*refs, n_extra, n_out, epi):
    a_ref, b_ref = refs[0], refs[1]
    extra = refs[2:2 + n_extra]
    outs = refs[2 + n_extra:2 + n_extra + n_out]
    acc_ref = refs[-1]
    k = pl.program_id(2)

    @pl.when(k == 0)
    def _():
        acc_ref[...] = jnp.zeros_like(acc_ref)

    acc_ref[...] += jnp.dot(a_ref[...], b_ref[...], preferred_element_type=F32)

    @pl.when(k == pl.num_programs(2) - 1)
    def _():
        epi(acc_ref[...], extra, outs)


def matmul(a, b, *, epi, out_shapes, out_specs, tm, tn, tk=None, extra=(), extra_specs=(),
           n_row_blocks=None, row_block_off=0, name="matmul"):
    M, K = a.shape
    N = b.shape[1]
    if tk is None:
        tk = K if 2 * K * tn * 2 <= MM_WEIGHT_BYTES else _tile(K, 1024, LANES)
    nrb = M // tm if n_row_blocks is None else n_row_blocks
    grid = (nrb, N // tn, K // tk)
    return pl.pallas_call(
        functools.partial(_mm_body, n_extra=len(extra), n_out=len(out_shapes), epi=epi),
        grid=grid,
        in_specs=[pl.BlockSpec((tm, tk), lambda i, j, k: (i + row_block_off, k)),
                  pl.BlockSpec((tk, tn), lambda i, j, k: (k, j))] + list(extra_specs),
        out_specs=list(out_specs), out_shape=list(out_shapes),
        scratch_shapes=[pltpu.VMEM((tm, tn), F32)],
        compiler_params=_params(("parallel", "parallel", "arbitrary")), name=name)(a, b, *extra)


def _epi_plain(acc, extra, outs):
    outs[0][...] = acc.astype(outs[0].dtype)


def matmul_plain(a, b, out_dtype=F32, name="matmul"):
    M, N = a.shape[0], b.shape[1]
    tn = _tile(N, 1024, 2 * LANES)
    tm = _tile(M, MM_ACC_ELEMS // tn, 2 * SUBLANES)
    return matmul(a, b, epi=_epi_plain, out_shapes=[jax.ShapeDtypeStruct((M, N), out_dtype)],
                  out_specs=[pl.BlockSpec((tm, tn), lambda i, j, k: (i, j))], tm=tm, tn=tn, name=name)[0]


def _rope_mix(t, tab):
    lane = lax.broadcasted_iota(jnp.int32, t.shape, 1)
    ss = jnp.sum(jnp.where(lane < ROPE, t * t, 0.0), axis=-1, keepdims=True)
    m = t * tab
    return (m + pltpu.roll(m, ROPE, axis=1)) * lax.rsqrt(ss * (1.0 / ROPE) + RMS_EPS)


def _epi_mq(acc, extra, outs):
    outs[0][...] = _rms(acc, extra[0][...]).astype(outs[0].dtype)


def _epi_mkv(acc, extra, outs):
    g_ref, tab_ref = extra
    outs[0][...] = _rms(acc[:, :KV_LORA], g_ref[...]).astype(outs[0].dtype)
    outs[1][...] = _rope_mix(acc[:, KV_LORA:], tab_ref[...])[:, :ROPE]


def _epi_qup(acc, extra, outs):
    g_ref, tab_ref = extra
    o = outs[0]
    for h in range(o.shape[0]):
        a = acc[:, h * QH:(h + 1) * QH]
        o[h, :, :NOPE] = (_rms(a[:, :NOPE], g_ref[...]) * Q_SCALE).astype(o.dtype)
        o[h, :, NOPE:] = (_rope_mix(a[:, NOPE:], tab_ref[...])[:, :ROPE] * Q_SCALE).astype(o.dtype)


def _epi_kvup(acc, extra, outs):
    g_ref, kr_ref = extra
    ok, ov = outs
    kr = kr_ref[...].astype(ok.dtype)
    for h in range(ok.shape[0]):
        a = acc[:, h * (NOPE + V_HEAD):(h + 1) * (NOPE + V_HEAD)]
        ok[h, :, :NOPE] = _rms(a[:, :NOPE], g_ref[...]).astype(ok.dtype)
        ok[h, :, NOPE:] = kr
        ov[h] = a[:, NOPE:].astype(ov.dtype)


def _epi_res(acc, extra, outs):
    res_ref, gate_ref = extra
    outs[0][...] = res_ref[...] + gate_ref[...] * acc


def mla_q_up(cq, w, qn_g, tab, B, Lx, row_off):
    tm = _tile(Lx, 512, 2 * SUBLANES)
    nt = Lx // tm
    hg = MLA_HEAD_GROUP
    return matmul(
        cq, w, epi=_epi_qup, tm=tm, tn=hg * QH, tk=Q_LORA,
        extra=(qn_g.reshape(1, NOPE), tab),
        extra_specs=(pl.BlockSpec((1, NOPE), lambda i, j, k: (0, 0)),
                     pl.BlockSpec((tm, 2 * ROPE), lambda i, j, k: (i % nt, 0))),
        out_shapes=[jax.ShapeDtypeStruct((B, N_HEADS_M, Lx, KD), BF16)],
        out_specs=[pl.BlockSpec((None, hg, tm, KD), lambda i, j, k: (i // nt, j, i % nt, 0))],
        n_row_blocks=B * nt, row_block_off=row_off // tm, name="mla_q_up")[0]


def mla_kv_up(ckv, w, kn_g, kr, B, Lx, row_off):
    tm = _tile(Lx, 512, 2 * SUBLANES)
    nt = Lx // tm
    rbo = row_off // tm
    hg = MLA_HEAD_GROUP
    return matmul(
        ckv, w, epi=_epi_kvup, tm=tm, tn=hg * (NOPE + V_HEAD), tk=KV_LORA,
        extra=(kn_g.reshape(1, NOPE), kr),
        extra_specs=(pl.BlockSpec((1, NOPE), lambda i, j, k: (0, 0)),
                     pl.BlockSpec((tm, ROPE), lambda i, j, k: (i + rbo, 0))),
        out_shapes=[jax.ShapeDtypeStruct((B, N_HEADS_M, Lx, KD), BF16),
                    jax.ShapeDtypeStruct((B, N_HEADS_M, Lx, V_HEAD), BF16)],
        out_specs=[pl.BlockSpec((None, hg, tm, KD), lambda i, j, k: (i // nt, j, i % nt, 0)),
                   pl.BlockSpec((None, hg, tm, V_HEAD), lambda i, j, k: (i // nt, j, i % nt, 0))],
        n_row_blocks=B * nt, row_block_off=rbo, name="mla_kv_up")


def _outproj_body(a0_ref, a1_ref, a2_ref, b_ref, res_ref, gate_ref, o_ref, *, offs):
    acc = None
    for a_ref, lo in zip((a0_ref, a1_ref, a2_ref), offs):
        d = jnp.dot(a_ref[...], b_ref[pl.ds(lo, a_ref.shape[1]), :], preferred_element_type=F32)
        acc = d if acc is None else acc + d
    o_ref[...] = res_ref[...] + gate_ref[...] * acc


def out_proj(parts, w, res, gate, rows, rows_per_mod):
    K, N = w.shape
    R = gate.shape[0]
    tn = _tile(N, 1024, 2 * LANES)
    tm = _tile(math.gcd(rows, rows_per_mod), MM_ACC_ELEMS // tn, 2 * SUBLANES)
    offs, lo = [], 0
    for p in parts:
        offs.append(lo)
        lo += p.shape[1]
    assert lo == K
    o_spec = pl.BlockSpec((tm, tn), lambda i, j: (i, j))
    return pl.pallas_call(
        functools.partial(_outproj_body, offs=tuple(offs)),
        grid=(rows // tm, N // tn),
        in_specs=[pl.BlockSpec((tm, p.shape[1]), lambda i, j: (i, 0)) for p in parts] + [
            pl.BlockSpec((K, tn), lambda i, j: (0, j)), o_spec,
            pl.BlockSpec((None, 1, tn), lambda i, j: (jnp.minimum(i * tm // rows_per_mod, R - 1), 0, j))],
        out_specs=o_spec,
        out_shape=jax.ShapeDtypeStruct((rows, N), F32),
        compiler_params=_params(("parallel", "parallel")),
        name="out_proj")(*parts, w, res, gate.reshape(R, 1, N))


def _seq_row_block(b, i, t, L, Lc, nlat):
    nc = Lc // t
    return jnp.where(i < nc, (nlat + b * Lc) // t + i, (b * L) // t + (i - nc))


def _conv_body(u_ref, up_ref, un_ref, dw_ref, db_ref, g_ref, b_ref, o_ref, buf_ref, sh_ref, *, tl, rc, nc, i0):
    i = pl.program_id(1) + i0
    nt = pl.num_programs(1) + i0
    first_tile = (i == 0) | (i == nc)
    last_tile = (i == nc - 1) | (i == nt - 1)

    def glu(u):
        return u[:, :C_CONV] * jax.nn.sigmoid(u[:, C_CONV:])

    buf_ref[pl.ds(0, CONV_HALO), :] = jnp.where(first_tile, 0.0, glu(up_ref[...]))
    buf_ref[pl.ds(CONV_HALO, tl), :] = glu(u_ref[...])
    buf_ref[pl.ds(CONV_HALO + tl, CONV_HALO), :] = jnp.where(last_tile, 0.0, glu(un_ref[...]))
    nsh = tl + 2 * CONV_HALO - SUBLANES
    for s in range(1, SUBLANES):
        sh_ref[s - 1] = buf_ref[pl.ds(s, nsh), :]

    first = CONV_HALO - CONV_W // 2

    def chunk(c, carry):
        r0 = pl.multiple_of(c * rc, rc)
        acc = jnp.broadcast_to(db_ref[...], (rc, C_CONV))
        for k in range(CONV_W):
            q, s = divmod(first + k, SUBLANES)
            start = pl.multiple_of(r0 + q * SUBLANES, SUBLANES)
            win = buf_ref[pl.ds(start, rc), :] if s == 0 else sh_ref[s - 1, pl.ds(start, rc), :]
            acc = acc + dw_ref[pl.ds(k, 1), :] * win
        mu = jnp.mean(acc, axis=-1, keepdims=True)
        d = acc - mu
        var = jnp.mean(d * d, axis=-1, keepdims=True)
        y = d * lax.rsqrt(var + LN_EPS) * g_ref[...] + b_ref[...]
        o_ref[pl.ds(r0, rc), :] = _silu(y).astype(o_ref.dtype)
        return carry

    lax.fori_loop(0, tl // rc, chunk, 0)


def conformer_conv(u, dw, db, ln_g, ln_b, B, L, Lc, with_ctx):
    T = u.shape[0]
    nlat = B * L
    tl = _tile(math.gcd(L, Lc), 256, CONV_HALO)
    rc = _tile(tl, 32, SUBLANES)
    nc = Lc // tl
    i0 = 0 if with_ctx else nc
    hb = tl // CONV_HALO
    nhalo = T // CONV_HALO
    row = lambda b, i: _seq_row_block(b, i + i0, tl, L, Lc, nlat)
    body = functools.partial(_conv_body, tl=tl, rc=rc, nc=nc, i0=i0)
    vec = lambda a: a.reshape(1, C_CONV)
    vspec = pl.BlockSpec((1, C_CONV), lambda b, i: (0, 0))
    return pl.pallas_call(
        body,
        grid=(B, (L + Lc) // tl - i0),
        in_specs=[pl.BlockSpec((tl, 2 * C_CONV), lambda b, i: (row(b, i), 0)),
                  pl.BlockSpec((CONV_HALO, 2 * C_CONV), lambda b, i: (jnp.maximum(row(b, i) * hb - 1, 0), 0)),
                  pl.BlockSpec((CONV_HALO, 2 * C_CONV), lambda b, i: (jnp.minimum((row(b, i) + 1) * hb, nhalo - 1), 0)),
                  pl.BlockSpec((CONV_W, C_CONV), lambda b, i: (0, 0)),
                  vspec, vspec, vspec],
        out_specs=pl.BlockSpec((tl, C_CONV), lambda b, i: (row(b, i), 0)),
        out_shape=jax.ShapeDtypeStruct((T if with_ctx else nlat, C_CONV), BF16),
        scratch_shapes=[pltpu.VMEM((tl + 2 * CONV_HALO, C_CONV), F32),
                        pltpu.VMEM((SUBLANES - 1, tl + 2 * CONV_HALO - SUBLANES, C_CONV), F32)],
        compiler_params=_params(("parallel", "arbitrary")),
        name="conformer_conv",
    )(u, u, u, dw, vec(db), vec(ln_g), vec(ln_b))


def _softplus(z):
    return jnp.maximum(z, 0.0) + jnp.log(1.0 + jnp.exp(-jnp.abs(z)))


def _feat_body(pr_ref, prv_ref, nxt_ref, mup_ref, mun_ref, w0_ref, w2_ref, a0_ref, a2_ref, kkp_ref, ka_ref,
               rk_ref, g2_ref, bd_ref, p1f_ref, p1b_ref, p2f_ref, p2b_ref, p3_ref, bonus_ref, gate_ref, buf_ref,
               *, tm, nc):
    i = pl.program_id(1)
    nt = pl.num_programs(1)
    first_tile = (i == 0) | (i == nc)
    last_tile = (i == nc - 1) | (i == nt - 1)
    buf_ref[pl.ds(SUBLANES, tm), :] = pr_ref[...]
    buf_ref[pl.ds(0, SUBLANES), :] = jnp.where(first_tile, 0.0, prv_ref[...])
    buf_ref[pl.ds(SUBLANES + tm, SUBLANES), :] = jnp.where(last_tile, 0.0, nxt_ref[...])

    def shifted(lo, hi):
        p = buf_ref[pl.ds(SUBLANES, tm), lo:hi]
        pv = buf_ref[pl.ds(SUBLANES - 1, tm), lo:hi]
        nx = buf_ref[pl.ds(SUBLANES + 1, tm), lo:hi]
        return p + mup_ref[:, lo:hi] * (pv - p) + mun_ref[:, lo:hi] * (nx - p)

    def head_sum(x):
        return jnp.concatenate(
            [jnp.dot(x[:, c * LANES:(c + 1) * LANES], bd_ref[...], precision=HP, preferred_element_type=F32)
             for c in range(C_R // LANES)], axis=-1)

    r = shifted(RC_R, RC_K)
    k = shifted(RC_K, RC_V)
    v = shifted(RC_V, RC_W)
    wd = jnp.tanh(shifted(RC_W, RC_A))
    ad = shifted(RC_A, RC_G)
    rg = shifted(RC_G, RCOLS)
    wl = -_softplus(-(w0_ref[...] + jnp.dot(wd, w2_ref[...], precision=HP, preferred_element_type=F32))) - 0.5
    dec = jnp.exp(-jnp.exp(wl))
    a = jax.nn.sigmoid(a0_ref[...] + jnp.dot(ad, a2_ref[...], precision=HP, preferred_element_type=F32))
    kkv = k * kkp_ref[...]
    kk = kkv / jnp.maximum(jnp.sqrt(head_sum(kkv * kkv)), 1e-12)
    gate_ref[...] = jnp.dot(jax.nn.sigmoid(rg), g2_ref[...], precision=HP, preferred_element_type=F32)

    lane = lax.broadcasted_iota(jnp.int32, (tm, LANES), 1)
    low = lane < HEAD_R

    def pack(o_ref, x, y):
        heads = []
        for c in range(C_R // LANES):
            xa = x[:, c * LANES:(c + 1) * LANES]
            ya = y[:, c * LANES:(c + 1) * LANES]
            heads.append(jnp.where(low, xa, pltpu.roll(ya, HEAD_R, axis=1)))
            heads.append(jnp.where(low, pltpu.roll(xa, HEAD_R, axis=1), ya))
        o_ref[...] = pltpu.einshape("htl->thl", jnp.stack(heads, axis=0))

    bonus = jnp.zeros((tm, C_R), F32)
    for d, (o1, o2) in enumerate(((p1f_ref, p2f_ref), (p1b_ref, p2b_ref))):
        a_d = a[:, d * C_R:(d + 1) * C_R]
        k_d = k * (1.0 + (a_d - 1.0) * ka_ref[...])
        pack(o1, dec[:, d * C_R:(d + 1) * C_R], kk * a_d)
        pack(o2, k_d, kk)
        bonus = bonus + head_sum(r * k_d * rk_ref[...]) * v
    pack(p3_ref, v, r)
    bonus_ref[...] = bonus


def rwkv_features(pr, rp, B, L, Lc):
    T = pr.shape[0]
    nlat = B * L
    Lt = L + Lc
    tm = _tile(math.gcd(L, Lc), 256, SUBLANES)
    nc = Lc // tm
    nrow8 = T // SUBLANES
    hb = tm // SUBLANES
    row = lambda b, i: _seq_row_block(b, i, tm, L, Lc, nlat)
    full = lambda a: pl.BlockSpec(a.shape, lambda b, i: (0,) * a.ndim)
    consts = (rp["mu_prev"], rp["mu_next"], rp["w0"], rp["w2"], rp["a0"], rp["a2"], rp["kk"], rp["ka"], rp["rk"],
              rp["g2"], rp["bd"])
    pk_shape = jax.ShapeDtypeStruct((Lt, B * N_HEADS_R, 2 * HEAD_R), F32)
    pk_spec = pl.BlockSpec((tm, N_HEADS_R, 2 * HEAD_R), lambda b, i: (i, b, 0))
    tok_shape = jax.ShapeDtypeStruct((T, C_R), F32)
    tok_spec = pl.BlockSpec((tm, C_R), lambda b, i: (row(b, i), 0))
    return pl.pallas_call(
        functools.partial(_feat_body, tm=tm, nc=nc),
        grid=(B, Lt // tm),
        in_specs=[pl.BlockSpec((tm, RCOLS), lambda b, i: (row(b, i), 0)),
                  pl.BlockSpec((SUBLANES, RCOLS), lambda b, i: (jnp.maximum(row(b, i) * hb - 1, 0), 0)),
                  pl.BlockSpec((SUBLANES, RCOLS), lambda b, i: (jnp.minimum((row(b, i) + 1) * hb, nrow8 - 1), 0))]
                 + [full(a) for a in consts],
        out_specs=[pk_spec] * 5 + [tok_spec] * 2,
        out_shape=[pk_shape] * 5 + [tok_shape] * 2,
        scratch_shapes=[pltpu.VMEM((tm + 2 * SUBLANES, RCOLS), F32)],
        compiler_params=_params(("parallel", "arbitrary")),
        name="rwkv_features",
    )(pr, pr, pr, *consts)


def _fold8(parts):
    sub = lax.broadcasted_iota(jnp.int32, parts[0].shape, 0)
    d = 1
    while len(parts) > 1:
        m = (sub // d) % 2 == 0
        parts = [jnp.where(m, a, b) + pltpu.roll(jnp.where(m, b, a), d, axis=0)
                 for a, b in zip(parts[0::2], parts[1::2])]
        d *= 2
    return parts[0]


def _scan_body(p1f_ref, p1b_ref, p2f_ref, p2b_ref, p3f_ref, p3b_ref, yf_ref, yb_ref, s_ref, op_ref, ybuf_ref,
               *, tc, n, nch):
    @pl.when(pl.program_id(0) == 0)
    def _():
        s_ref[...] = jnp.zeros_like(s_ref)

    def prep(t, g_prev):
        tb = tc - 1 - t
        zt = [jnp.concatenate([f_ref[t], b_ref[tb]], axis=0).T
              for f_ref, b_ref in ((p1f_ref, p1b_ref), (p2f_ref, p2b_ref), (p3f_ref, p3b_ref))]
        g = g_prev * zt[0][:n]
        inv = 1.0 / g
        op_ref[t, 0] = zt[1][n:] * g_prev
        op_ref[t, 1] = zt[0][n:] * inv
        op_ref[t, 2] = zt[1][:n] * inv
        op_ref[t, 3] = zt[2][n:] * g
        op_ref[t, 4] = zt[2][:n]
        return g

    g_last = lax.fori_loop(0, tc, prep, jnp.ones((n, LANES), F32), unroll=4)
    nv = n // SUBLANES

    def colsum(x):
        return jnp.sum(x.reshape(nv, SUBLANES, x.shape[-1]), axis=0)

    def step(t, carry):
        kkg = op_ref[t, 0]
        kkai = op_ref[t, 1]
        ki = op_ref[t, 2]
        rg = op_ref[t, 3]

        def rows(ib, c2):
            base = pl.multiple_of(ib * SUBLANES, SUBLANES)
            sa8 = _fold8([colsum(s_ref[base + ii] * kkg) for ii in range(SUBLANES)])
            v8 = op_ref[t, 4, pl.ds(base, SUBLANES), :]
            py = []
            for ii in range(SUBLANES):
                s_new = s_ref[base + ii] - sa8[ii:ii + 1] * kkai + v8[ii:ii + 1] * ki
                s_ref[base + ii] = s_new
                py.append(colsum(s_new * rg))
            ybuf_ref[t, pl.ds(base, SUBLANES), :] = _fold8(py)
            return c2

        lax.fori_loop(0, nv, rows, 0, unroll=True)
        return carry

    lax.fori_loop(0, tc, step, 0)

    def rescale(i, carry):
        s_ref[i] = s_ref[i] * g_last
        return carry

    lax.fori_loop(0, n, rescale, 0, unroll=4)

    def fin(t, carry):
        y = ybuf_ref[t]
        yt = jnp.concatenate([y, y], axis=0).T
        yf_ref[t] = yt[:nch, :n]
        yb_ref[tc - 1 - t] = yt[nch:, :n]
        return carry

    lax.fori_loop(0, tc, fin, 0, unroll=4)


def wkv_scan(p1f, p1b, p2f, p2b, p3, Lc):
    Lt, nch, _ = p1f.shape
    n = HEAD_R
    assert 2 * nch == LANES and 2 * n == LANES
    tc = _tile(math.gcd(Lc, Lt - Lc), 32, 1)
    ncb, nb = Lc // tc, Lt // tc
    fwd = lambda g: (g, 0, 0)
    bwd = lambda g: (jnp.where(g < ncb, ncb - 1 - g, nb - 1 - (g - ncb)), 0, 0)
    ispec = lambda m: pl.BlockSpec((tc, nch, 2 * n), m)
    ospec = lambda m: pl.BlockSpec((tc, nch, n), m)
    y_shape = jax.ShapeDtypeStruct((Lt, nch, n), F32)
    return pl.pallas_call(
        functools.partial(_scan_body, tc=tc, n=n, nch=nch),
        grid=(nb,),
        in_specs=[ispec(fwd), ispec(bwd), ispec(fwd), ispec(bwd), ispec(fwd), ispec(bwd)],
        out_specs=[ospec(fwd), ospec(bwd)],
        out_shape=[y_shape, y_shape],
        scratch_shapes=[pltpu.VMEM((n, n, LANES), F32), pltpu.VMEM((tc, 5, n, LANES), F32),
                        pltpu.VMEM((tc, n, LANES), F32)],
        compiler_params=_params(("arbitrary",)),
        name="wkv_scan",
    )(p1f, p1b, p2f, p2b, p3, p3)


def _post_body(yf_ref, yb_ref, bonus_ref, gate_ref, g_ref, b_ref, o_ref):
    yh = pltpu.einshape("thl->htl", yf_ref[...] + yb_ref[...])
    cols = []
    for h in range(N_HEADS_R):
        y = yh[h]
        mu = jnp.mean(y, axis=-1, keepdims=True)
        d = y - mu
        var = jnp.mean(d * d, axis=-1, keepdims=True)
        cols.append(d * lax.rsqrt(var + GN_EPS))
    yn = jnp.concatenate(cols, axis=-1) * g_ref[...] + b_ref[...]
    o_ref[...] = ((yn + bonus_ref[...]) * gate_ref[...]).astype(o_ref.dtype)


def rwkv_output(yf, yb, bonus, gate, ln_g, ln_b, B, L, Lc, with_ctx):
    T = bonus.shape[0]
    nlat = B * L
    tm = _tile(math.gcd(L, Lc), 256, 2 * SUBLANES)
    i0 = 0 if with_ctx else Lc // tm
    row = lambda b, i: _seq_row_block(b, i + i0, tm, L, Lc, nlat)
    y_spec = pl.BlockSpec((tm, N_HEADS_R, HEAD_R), lambda b, i: (i + i0, b, 0))
    tok_spec = pl.BlockSpec((tm, C_R), lambda b, i: (row(b, i), 0))
    vspec = pl.BlockSpec((1, C_R), lambda b, i: (0, 0))
    return pl.pallas_call(
        _post_body,
        grid=(B, (L + Lc) // tm - i0),
        in_specs=[y_spec, y_spec, tok_spec, tok_spec, vspec, vspec],
        out_specs=tok_spec,
        out_shape=jax.ShapeDtypeStruct((T if with_ctx else nlat, C_R), BF16),
        compiler_params=_params(("parallel", "parallel")),
        name="rwkv_output",
    )(yf, yb, bonus, gate, ln_g.reshape(1, C_R), ln_b.reshape(1, C_R))


def _attn_body(*refs, n_kv, hq):
    kv = refs[1:1 + 2 * n_kv]
    o_ref = refs[-1]
    for a in range(refs[0].shape[0] // hq):
        q = refs[0][pl.ds(a * hq, hq), :]
        s = [lax.dot_general(q, kv[2 * j][...], (((1,), (1,)), ((), ())), preferred_element_type=F32)
             for j in range(n_kv)]
        m = functools.reduce(jnp.maximum, [jnp.max(x, axis=-1, keepdims=True) for x in s])
        p = [jnp.exp2(x - m) for x in s]
        l = functools.reduce(jnp.add, [jnp.sum(x, axis=-1, keepdims=True) for x in p])
        o = functools.reduce(jnp.add, [jnp.dot(p[j].astype(BF16), kv[2 * j + 1][...], preferred_element_type=F32)
                                       for j in range(n_kv)])
        o_ref[pl.ds(a * hq, hq), :] = (o / l).astype(o_ref.dtype)


def mla_attention(q, kvs, rows_total, row_off, prev=None):
    B, H, Lq, dk = q.shape
    dv = kvs[0][1].shape[3]
    hq = _tile(Lq, ATTN_SUBTILE, 2 * SUBLANES)
    tq = _tile(Lq, 2 * hq, hq)
    nq = Lq // tq
    rbo = row_off // tq
    in_specs = [pl.BlockSpec((None, None, tq, dk), lambda b, h, i: (b, h, i, 0))]
    args = [q]
    for k, v in kvs:
        in_specs += [pl.BlockSpec((None, None, k.shape[2], dk), lambda b, h, i: (b, h, 0, 0)),
                     pl.BlockSpec((None, None, v.shape[2], dv), lambda b, h, i: (b, h, 0, 0))]
        args += [k, v]
    body = functools.partial(_attn_body, n_kv=len(kvs), hq=hq)
    aliases = {}
    if prev is not None:
        in_specs.append(pl.BlockSpec(memory_space=pl.ANY))
        args.append(prev)
        aliases = {len(args) - 1: 0}
        body = lambda *refs: _attn_body(*refs[:-2], refs[-1], n_kv=len(kvs), hq=hq)
    return pl.pallas_call(
        body,
        grid=(B, H, nq),
        in_specs=in_specs,
        out_specs=pl.BlockSpec((tq, dv), lambda b, h, i: (rbo + b * nq + i, h)),
        out_shape=jax.ShapeDtypeStruct((rows_total, H * dv), BF16),
        input_output_aliases=aliases,
        compiler_params=_params(("parallel", "parallel", "arbitrary")),
        name="mla_attention",
    )(*args)


def _moe_body(be_ref, nu_ref, x_ref, w1_ref, w3_ref, w2_ref, o_ref):
    blk = pl.program_id(0)
    f = pl.program_id(1)

    @pl.when(f == 0)
    def _():
        o_ref[...] = jnp.zeros_like(o_ref)

    @pl.when(blk < nu_ref[0])
    def _():
        x = x_ref[...]
        h1 = jnp.dot(x, w1_ref[...], preferred_element_type=F32)
        h3 = jnp.dot(x, w3_ref[...], preferred_element_type=F32)
        hb = (_silu(h1) * h3).astype(BF16)
        o_ref[...] += jnp.dot(hb, w2_ref[...], preferred_element_type=F32)


def moe_experts(xs, block_e, n_used, w1, w3, w2):
    P, D = xs.shape
    Fd = w1.shape[2]
    tf = _tile(Fd, 512, LANES)
    nf = Fd // tf
    nb = P // MOE_ROWS

    def fidx(i, f, nu):
        return jnp.where(i < nu[0], f, nf - 1)

    grid_spec = pltpu.PrefetchScalarGridSpec(
        num_scalar_prefetch=2,
        grid=(nb, nf),
        in_specs=[pl.BlockSpec((MOE_ROWS, D), lambda i, f, be, nu: (i, 0)),
                  pl.BlockSpec((None, D, tf), lambda i, f, be, nu: (be[i], 0, fidx(i, f, nu))),
                  pl.BlockSpec((None, D, tf), lambda i, f, be, nu: (be[i], 0, fidx(i, f, nu))),
                  pl.BlockSpec((None, tf, D), lambda i, f, be, nu: (be[i], fidx(i, f, nu), 0))],
        out_specs=pl.BlockSpec((MOE_ROWS, D), lambda i, f, be, nu: (i, 0)),
    )
    return pl.pallas_call(
        _moe_body,
        grid_spec=grid_spec,
        out_shape=jax.ShapeDtypeStruct((P, D), F32),
        compiler_params=_params(("arbitrary", "arbitrary")),
        name="moe_experts",
    )(block_e, n_used, xs, w1, w3, w2)


def _gather_body(*refs, n_src, tm, combine):
    idx_refs = refs[:n_src]
    src_ref = refs[n_src]
    rest = refs[n_src + 1:]
    if combine:
        w_ref, base_ref, gate_ref, o_ref, buf_ref, sem_ref = rest
    else:
        o_ref, buf_ref, sem_ref = rest
    i = pl.program_id(0)
    nsteps = pl.num_programs(0)
    slot = i % 2

    def row_copy(step, r, j, sl):
        row = idx_refs[j][step * tm + r]
        return pltpu.make_async_copy(src_ref.at[pl.ds(row, 1), :], buf_ref.at[sl, j, pl.ds(r, 1), :], sem_ref.at[sl])

    def start_all(step, sl):
        def body(r, c):
            for j in range(n_src):
                row_copy(step, r, j, sl).start()
            return c
        lax.fori_loop(0, tm, body, 0, unroll=8)

    def wait_all(step, sl):
        def body(r, c):
            for j in range(n_src):
                row_copy(step, r, j, sl).wait()
            return c
        lax.fori_loop(0, tm, body, 0, unroll=8)

    @pl.when(i == 0)
    def _():
        start_all(0, 0)

    @pl.when(i + 1 < nsteps)
    def _():
        start_all(i + 1, 1 - slot)

    wait_all(i, slot)
    if combine:
        acc = buf_ref[slot, 0] * w_ref[:, 0:1]
        for j in range(1, n_src):
            acc = acc + buf_ref[slot, j] * w_ref[:, j:j + 1]
        o_ref[...] = base_ref[...] + gate_ref[...] * acc
    else:
        o_ref[...] = buf_ref[slot, 0].astype(o_ref.dtype)


def gather_rows(src, idx, out_dtype):
    R = idx.shape[0]
    D = src.shape[1]
    tm = _tile(R, GATHER_ROWS, 2 * SUBLANES)
    grid_spec = pltpu.PrefetchScalarGridSpec(
        num_scalar_prefetch=1, grid=(R // tm,),
        in_specs=[pl.BlockSpec(memory_space=pl.ANY)],
        out_specs=pl.BlockSpec((tm, D), lambda i, ix: (i, 0)),
        scratch_shapes=[pltpu.VMEM((2, 1, tm, D), F32), pltpu.SemaphoreType.DMA((2,))])
    return pl.pallas_call(
        functools.partial(_gather_body, n_src=1, tm=tm, combine=False), grid_spec=grid_spec,
        out_shape=jax.ShapeDtypeStruct((R, D), out_dtype),
        compiler_params=_params(("arbitrary",)), name="moe_dispatch")(idx, src)


def combine_rows(src, idxs, w, base, gate, rows_per_mod):
    T, D = base.shape
    n_src = len(idxs)
    Rg = gate.shape[0]
    tm = _tile(math.gcd(T, rows_per_mod), GATHER_ROWS // n_src, SUBLANES)
    grid_spec = pltpu.PrefetchScalarGridSpec(
        num_scalar_prefetch=n_src, grid=(T // tm,),
        in_specs=[pl.BlockSpec(memory_space=pl.ANY),
                  pl.BlockSpec((tm, n_src), lambda i, *ix: (i, 0)),
                  pl.BlockSpec((tm, D), lambda i, *ix: (i, 0)),
                  pl.BlockSpec((None, 1, D), lambda i, *ix: (jnp.minimum(i * tm // rows_per_mod, Rg - 1), 0, 0))],
        out_specs=pl.BlockSpec((tm, D), lambda i, *ix: (i, 0)),
        scratch_shapes=[pltpu.VMEM((2, n_src, tm, D), F32), pltpu.SemaphoreType.DMA((2,))])
    return pl.pallas_call(
        functools.partial(_gather_body, n_src=n_src, tm=tm, combine=True), grid_spec=grid_spec,
        out_shape=jax.ShapeDtypeStruct((T, D), F32),
        compiler_params=_params(("arbitrary",)), name="moe_combine")(*idxs, src, w, base, gate.reshape(Rg, 1, D))


def _top2(x):
    pos = lax.broadcasted_iota(jnp.int32, x.shape, x.ndim - 1)
    i1 = jnp.argmax(x, -1)
    m1 = jnp.max(x, -1)
    x2 = jnp.where(pos == i1[..., None], -jnp.inf, x)
    return jnp.stack([m1, jnp.max(x2, -1)], -1), jnp.stack([i1, jnp.argmax(x2, -1)], -1)


def _route(logits, router_bias):
    scores = jax.nn.sigmoid(logits)
    biased = (scores + router_bias.astype(F32)).reshape(-1, N_GROUPS, EXPERTS_PER_GROUP)
    group_score = jnp.sum(_top2(biased)[0], -1)
    chosen = jnp.argmax(group_score, -1)
    in_group = jnp.arange(N_GROUPS)[None, :] == chosen[:, None]
    masked = jnp.where(in_group[:, :, None], biased, -jnp.inf).reshape(-1, N_EXPERTS)
    _, idx = _top2(masked)
    w = jnp.take_along_axis(scores, idx, -1)
    return idx, w / jnp.sum(w, -1, keepdims=True)


def moe_ffn(tok, logits, router_bias, w1, w3, w2, base, gate, rows_per_mod):
    T, D = tok.shape
    idx, wts = _route(logits, router_bias)
    M = T * TOP_K
    flat_e = idx.reshape(-1).astype(jnp.int32)
    onehot = (flat_e[:, None] == jnp.arange(N_EXPERTS, dtype=jnp.int32)[None, :]).astype(jnp.int32)
    csum = jnp.cumsum(onehot, axis=0)
    rank = jnp.take_along_axis(csum, flat_e[:, None], axis=1)[:, 0] - 1
    counts = csum[-1]
    pcounts = (counts + MOE_ROWS - 1) // MOE_ROWS * MOE_ROWS
    pends = jnp.cumsum(pcounts)
    pstarts = pends - pcounts
    dest = pstarts[flat_e] + rank
    n_blocks = -(-M // MOE_ROWS) + N_EXPERTS
    P = n_blocks * MOE_ROWS
    flat_t = jnp.repeat(jnp.arange(T, dtype=jnp.int32), TOP_K)
    row_tok = jnp.zeros((P,), jnp.int32).at[dest].set(flat_t)
    n_used = (pends[-1] // MOE_ROWS).astype(jnp.int32)
    blk_start = jnp.arange(n_blocks, dtype=jnp.int32) * MOE_ROWS
    block_e = jnp.clip(jnp.searchsorted(pends, blk_start, side="right"), 0, N_EXPERTS - 1).astype(jnp.int32)
    last_e = block_e[jnp.maximum(n_used - 1, 0)]
    block_e = jnp.where(jnp.arange(n_blocks) < n_used, block_e, last_e)
    xs = gather_rows(tok, row_tok, BF16)
    yb = moe_experts(xs, block_e, n_used.reshape(1), w1, w3, w2)
    dest = dest.reshape(T, TOP_K)
    return combine_rows(yb, [dest[:, j] for j in range(TOP_K)], wts, base, gate, rows_per_mod)


def _rope_tables(rows):
    half = ROPE // 2
    inv_freq = ROPE_BASE ** (-jnp.arange(0, half, 2, dtype=F32) / half)
    r = jnp.repeat(jnp.arange(rows, dtype=F32), GRID_W)
    cl = jnp.tile(jnp.arange(GRID_W, dtype=F32), rows)
    ar = r[:, None] * inv_freq
    ac = cl[:, None] * inv_freq
    ang = jnp.concatenate([ar, ar, ac, ac], -1)
    return jnp.cos(ang), jnp.sin(ang)


def _rot_cols(w):
    q = ROPE // 4
    return jnp.concatenate([-w[..., q:2 * q], w[..., :q], -w[..., 3 * q:], w[..., 2 * q:3 * q]], -1)


def _rot_perm(g):
    q = ROPE // 4
    return jnp.concatenate([g[q:2 * q], g[:q], g[3 * q:], g[2 * q:3 * q]], -1)


def _rope_tab(g, cos, sin):
    if cos is None:
        return jnp.concatenate([g, jnp.zeros_like(g)])[None, :]
    return jnp.concatenate([g[None, :] * cos, _rot_perm(g)[None, :] * sin], -1)


def _blockdiag2(w):
    z = jnp.zeros_like(w[0])
    return jnp.concatenate([jnp.concatenate([w[0], z], 1), jnp.concatenate([z, w[1]], 1)], 0)


def kernel(x, c, ctx, c_ctx, w_mod, b_mod, norm1_g, norm2_g, w_in, w_out, conv_dw, conv_b, conv_ln_g, conv_ln_b, r_mu, r_w0, r_w2, r_a0, r_a2, r_g2, r_kk, r_ka, r_rk, r_ln_g, r_ln_b, m_cq_g, m_w_uq, m_ckv_g, m_w_ukv, m_qn_g, m_qr_g, m_kn_g, m_kr_g, w_router, router_bias, moe_w1, moe_w3, moe_w2):
    B, L, D = x.shape
    Lc = ctx.shape[1]
    depth = w_mod.shape[0]
    nlat, nctx = B * L, B * Lc
    T = nlat + nctx
    cos, sin = _rope_tables(L // GRID_W)
    tm_in = _tile(math.gcd(L, nctx), 512, 2 * SUBLANES)
    bd = jnp.kron(jnp.eye(LANES // HEAD_R, dtype=F32), jnp.ones((HEAD_R, HEAD_R), F32))

    H = jnp.concatenate([x.reshape(nlat, D), ctx.reshape(nctx, D)], 0)
    crow = jnp.concatenate([c, c_ctx[None, :], jnp.zeros((SUBLANES - (B + 1) % SUBLANES, D), F32)], 0)

    for l in range(depth):
        last = l == depth - 1
        mod = adaln_mod(crow, w_mod, b_mod, l)[:B + 1]
        sh1, sc1, ga1, sh2, sc2, ga2 = jnp.split(mod, 6, axis=-1)
        n = norm_mod(H, norm1_g[l], sc1, sh1, L)

        wi = w_in[l]
        cols = lambda lo, hi: wi[:, lo:hi]
        w_conv = cast_bf16(w_in, l, OFF_CONV, OFF_MQ - OFF_CONV)
        w_mq = cast_bf16(w_in, l, OFF_MQ, OFF_RR - OFF_MQ)
        w_rw = cast_bf16(jnp.concatenate(
            [cols(OFF_RR, OFF_RG), cols(OFF_RK, OFF_RV), cols(OFF_RV, OFF_RW), cols(OFF_RW, OFF_RA),
             cols(OFF_RA, OFF_MKV), cols(OFF_RG, OFF_RK), jnp.zeros((D, G_PAD - G_LORA), F32)], 1))
        w_kv = cast_bf16(jnp.concatenate(
            [cols(OFF_MKV, OFF_MKR), cols(OFF_MKR, IN_COLS), _rot_cols(cols(OFF_MKR, IN_COLS))], 1))
        u = matmul_plain(n, w_conv, name="in_conv")
        pr = matmul_plain(n, w_rw, name="in_rwkv")
        cq = matmul(n, w_mq, epi=_epi_mq, tm=tm_in, tn=Q_LORA,
                    extra=(m_cq_g[l].reshape(1, Q_LORA),),
                    extra_specs=(pl.BlockSpec((1, Q_LORA), lambda i, j, k: (0, 0)),),
                    out_shapes=[jax.ShapeDtypeStruct((T, Q_LORA), BF16)],
                    out_specs=[pl.BlockSpec((tm_in, Q_LORA), lambda i, j, k: (i, 0))], name="in_mq")[0]
        ktab = jnp.concatenate([_rope_tab(m_kr_g[l], cos, sin),
                                jnp.broadcast_to(_rope_tab(m_kr_g[l], None, None), (tm_in, 2 * ROPE))], 0)
        nlb = L // tm_in
        ckv, kr = matmul(n, w_kv, epi=_epi_mkv, tm=tm_in, tn=KV_LORA + 2 * ROPE,
                         extra=(m_ckv_g[l].reshape(1, KV_LORA), ktab),
                         extra_specs=(pl.BlockSpec((1, KV_LORA), lambda i, j, k: (0, 0)),
                                      pl.BlockSpec((tm_in, 2 * ROPE),
                                                   lambda i, j, k: (jnp.where(i < B * nlb, i % nlb, nlb), 0))),
                         out_shapes=[jax.ShapeDtypeStruct((T, KV_LORA), BF16), jax.ShapeDtypeStruct((T, ROPE), F32)],
                         out_specs=[pl.BlockSpec((tm_in, KV_LORA), lambda i, j, k: (i, 0)),
                                    pl.BlockSpec((tm_in, ROPE), lambda i, j, k: (i, 0))], name="in_mkv")

        conv = conformer_conv(u, conv_dw[l], conv_b[l], conv_ln_g[l], conv_ln_b[l], B, L, Lc, not last)

        mu = r_mu[l]
        mcols = lambda lo, hi: mu[:, lo - OFF_RR:hi - OFF_RR]
        mu_r = jnp.concatenate([mcols(OFF_RR, OFF_RG), mcols(OFF_RK, OFF_RV), mcols(OFF_RV, OFF_RW),
                                mcols(OFF_RW, OFF_RA), mcols(OFF_RA, OFF_MKV), mcols(OFF_RG, OFF_RK),
                                jnp.zeros((2, G_PAD - G_LORA), F32)], 1)
        rp = {"mu_prev": mu_r[0:1], "mu_next": mu_r[1:2],
              "w0": r_w0[l].reshape(1, 2 * C_R), "w2": _blockdiag2(r_w2[l]),
              "a0": r_a0[l].reshape(1, 2 * C_R), "a2": _blockdiag2(r_a2[l]),
              "kk": r_kk[l].reshape(1, C_R), "ka": r_ka[l].reshape(1, C_R), "rk": r_rk[l].reshape(1, C_R),
              "g2": jnp.pad(r_g2[l], ((0, G_PAD - G_LORA), (0, 0))), "bd": bd}
        p1f, p1b, p2f, p2b, p3, bonus, gate = rwkv_features(pr, rp, B, L, Lc)
        yf, yb = wkv_scan(p1f, p1b, p2f, p2b, p3, Lc)
        rw = rwkv_output(yf, yb, bonus, gate, r_ln_g[l], r_ln_b[l], B, L, Lc, not last)

        wq = m_w_uq[l].reshape(Q_LORA, N_HEADS_M, NOPE + ROPE)
        wq = cast_bf16(jnp.concatenate([wq, _rot_cols(wq[..., NOPE:])], -1).reshape(Q_LORA, N_HEADS_M * QH))
        wkv = cast_bf16(m_w_ukv, l)
        k_c, v_c = mla_kv_up(ckv, wkv, m_kn_g[l], kr, B, Lc, nlat)
        k_l, v_l = mla_kv_up(ckv, wkv, m_kn_g[l], kr, B, L, 0)
        q_l = mla_q_up(cq, wq, m_qn_g[l], _rope_tab(m_qr_g[l], cos, sin), B, L, 0)
        rows_out = nlat if last else T
        mla = mla_attention(q_l, [(k_c, v_c), (k_l, v_l)], rows_out, 0,
                            prev=None if last else jnp.zeros((T, C_M), BF16))
        if not last:
            tq_c = _tile(Lc, 512, 2 * SUBLANES)
            q_c = mla_q_up(cq, wq, m_qn_g[l], jnp.broadcast_to(_rope_tab(m_qr_g[l], None, None), (tq_c, 2 * ROPE)),
                           B, Lc, nlat)
            mla = mla_attention(q_c, [(k_c, v_c)], rows_out, nlat, prev=mla)

        Hn = out_proj((conv, rw, mla), cast_bf16(w_out, l), H, ga1, rows_out, L)
        n2, logits = norm_mod(Hn, norm2_g[l], sc2, sh2, L, w_router=w_router)
        H = moe_ffn(n2, logits[:, :N_EXPERTS], router_bias,
                    cast_bf16(moe_w1, l), cast_bf16(moe_w3, l), cast_bf16(moe_w2, l), Hn, ga2, L)
    return H[:nlat].reshape(B, L, D)
```

```python
import functools
import math

import jax
import jax.numpy as jnp
from jax import lax
from jax.experimental import pallas as pl
from jax.experimental.pallas import tpu as pltpu

GRID_W = 64
C_CONV = 1024
CONV_W = 31
N_HEADS_R = 16
HEAD_R = 64
C_R = N_HEADS_R * HEAD_R
W_LORA = 64
A_LORA = 64
G_LORA = 160
GN_EPS = 64e-5
N_HEADS_M = 16
NOPE = 128
ROPE = 64
V_HEAD = 128
Q_LORA = 1536
KV_LORA = 512
C_M = N_HEADS_M * V_HEAD
ROPE_BASE = 10000.0
ATTN_SCALE = (NOPE + ROPE) ** -0.5
Q_SCALE = ATTN_SCALE * math.log2(math.e)
RMS_EPS = 1e-6
LN_EPS = 1e-5

OFF_CONV = 0
OFF_MQ = OFF_CONV + 2 * C_CONV
OFF_RR = OFF_MQ + Q_LORA
OFF_RG = OFF_RR + C_R
OFF_RK = OFF_RG + G_LORA
OFF_RV = OFF_RK + C_R
OFF_RW = OFF_RV + C_R
OFF_RA = OFF_RW + 2 * W_LORA
OFF_MKV = OFF_RA + 2 * A_LORA
OFF_MKR = OFF_MKV + KV_LORA
IN_COLS = OFF_MKR + ROPE

N_EXPERTS = 16
N_GROUPS = 4
EXPERTS_PER_GROUP = N_EXPERTS // N_GROUPS
TOP_K = 2

LANES = 128
SUBLANES = 8
VMEM_LIMIT = 56 * 1024 * 1024
CONV_HALO = 16
MOE_ROWS = 512
CAST_BLOCK_ELEMS = 2 * 1024 * 1024
MM_ACC_ELEMS = 512 * 1024
MM_WEIGHT_BYTES = 28 * 1024 * 1024
GATHER_ROWS = 256
MLA_HEAD_GROUP = 4
ATTN_SUBTILE = 256

RC_R, RC_K, RC_V, RC_W, RC_A, RC_G = 0, C_R, 2 * C_R, 3 * C_R, 3 * C_R + 2 * W_LORA, 3 * C_R + 2 * W_LORA + 2 * A_LORA
G_PAD = 2 * LANES
RCOLS = RC_G + G_PAD
QH = NOPE + 2 * ROPE
KD = NOPE + ROPE

BF16 = jnp.bfloat16
F32 = jnp.float32
HP = lax.Precision.HIGHEST


def _params(sem):
    return pltpu.CompilerParams(dimension_semantics=sem, vmem_limit_bytes=VMEM_LIMIT)


def _tile(n, target, mult):
    best = None
    for d in range(mult, min(n, target) + 1, mult):
        if n % d == 0:
            best = d
    return n if best is None else best


def _silu(x):
    return x * jax.nn.sigmoid(x)


def _rms(x, g):
    return x * lax.rsqrt(jnp.mean(x * x, axis=-1, keepdims=True) + RMS_EPS) * g


def _cast_body(x_ref, o_ref):
    o_ref[...] = x_ref[...].astype(o_ref.dtype)


def cast_bf16(w, layer=None, col_off=0, cols=None):
    w3 = w[None] if layer is None else w
    lay = 0 if layer is None else layer
    lead = w3.shape[1:-1]
    w3 = w3.reshape(w3.shape[0], -1, w3.shape[-1])
    R = w3.shape[1]
    C = w3.shape[2] - col_off if cols is None else cols
    tc = _tile(math.gcd(C, col_off) if col_off else C, 4096, LANES)
    tr = _tile(R, max(CAST_BLOCK_ELEMS // tc, 2 * SUBLANES), 2 * SUBLANES)
    cb = col_off // tc
    out = pl.pallas_call(
        _cast_body, grid=(R // tr, C // tc),
        in_specs=[pl.BlockSpec((None, tr, tc), lambda i, j: (lay, i, j + cb))],
        out_specs=pl.BlockSpec((tr, tc), lambda i, j: (i, j)),
        out_shape=jax.ShapeDtypeStruct((R, C), BF16),
        compiler_params=_params(("parallel", "parallel")), name="cast_bf16")(w3)
    return out.reshape(lead + (C,))


def _mod_body(c_ref, w_ref, b_ref, o_ref):
    @pl.when(pl.program_id(0) == 0)
    def _():
        o_ref[...] = jnp.broadcast_to(b_ref[...], o_ref.shape)

    a = _silu(c_ref[...]).astype(BF16)
    o_ref[...] += jnp.dot(a, w_ref[...].astype(BF16), preferred_element_type=F32)


def adaln_mod(rows, w, b, layer):
    R, D = rows.shape
    N = w.shape[2]
    tk = LANES
    return pl.pallas_call(
        _mod_body,
        grid=(D // tk,),
        in_specs=[pl.BlockSpec((R, tk), lambda k: (0, k)),
                  pl.BlockSpec((None, tk, N), lambda k: (layer, k, 0)),
                  pl.BlockSpec((None, 1, N), lambda k: (layer, 0, 0))],
        out_specs=pl.BlockSpec((R, N), lambda k: (0, 0)),
        out_shape=jax.ShapeDtypeStruct((R, N), F32),
        compiler_params=_params(("arbitrary",)),
        name="adaln_mod",
    )(rows, w, b.reshape(b.shape[0], 1, N))


def _norm_mod_body(x_ref, g_ref, sc_ref, sh_ref, o_ref):
    o_ref[...] = (_rms(x_ref[...], g_ref[...]) * (1.0 + sc_ref[...]) + sh_ref[...]).astype(o_ref.dtype)


def _norm_mod_pair_body(xa_ref, xb_ref, g_ref, sc_ref, sh_ref, o_ref, *, na):
    x = jnp.where(pl.program_id(0) < na, xa_ref[...], xb_ref[...])
    o_ref[...] = (_rms(x, g_ref[...]) * (1.0 + sc_ref[...]) + sh_ref[...]).astype(o_ref.dtype)


def _norm_mod_router_body(x_ref, g_ref, sc_ref, sh_ref, wr_ref, o_ref, lg_ref):
    y = _rms(x_ref[...], g_ref[...]) * (1.0 + sc_ref[...]) + sh_ref[...]
    o_ref[...] = y.astype(o_ref.dtype)
    lg_ref[...] = jnp.dot(y, wr_ref[...], precision=HP, preferred_element_type=F32)


def norm_mod(h, g, sc, sh, rows_per_mod, w_router=None):
    R = sc.shape[0]
    if isinstance(h, tuple):
        xa, xb = h
        D = xa.shape[1]
        T = xa.shape[0] + xb.shape[0]
        tm = _tile(math.gcd(math.gcd(xa.shape[0], xb.shape[0]), rows_per_mod), 256, 2 * SUBLANES)
        na = xa.shape[0] // tm
        mod_row = lambda i: (jnp.minimum(i * tm // rows_per_mod, R - 1), 0, 0)
        return pl.pallas_call(
            functools.partial(_norm_mod_pair_body, na=na), grid=(T // tm,),
            in_specs=[pl.BlockSpec((tm, D), lambda i: (jnp.minimum(i, na - 1), 0)),
                      pl.BlockSpec((tm, D), lambda i: (jnp.maximum(i - na, 0), 0)),
                      pl.BlockSpec((1, D), lambda i: (0, 0)),
                      pl.BlockSpec((None, 1, D), mod_row), pl.BlockSpec((None, 1, D), mod_row)],
            out_specs=pl.BlockSpec((tm, D), lambda i: (i, 0)),
            out_shape=jax.ShapeDtypeStruct((T, D), BF16),
            compiler_params=_params(("parallel",)), name="norm_mod")(
                xa, xb, g.reshape(1, D), sc.reshape(R, 1, D), sh.reshape(R, 1, D))
    T, D = h.shape
    tm = _tile(math.gcd(T, rows_per_mod), 256, 2 * SUBLANES)
    mod_row = lambda i: (jnp.minimum(i * tm // rows_per_mod, R - 1), 0, 0)
    in_specs = [pl.BlockSpec((tm, D), lambda i: (i, 0)),
                pl.BlockSpec((1, D), lambda i: (0, 0)),
                pl.BlockSpec((None, 1, D), mod_row),
                pl.BlockSpec((None, 1, D), mod_row)]
    args = [h, g.reshape(1, D), sc.reshape(R, 1, D), sh.reshape(R, 1, D)]
    o_spec = pl.BlockSpec((tm, D), lambda i: (i, 0))
    o_shape = jax.ShapeDtypeStruct((T, D), BF16 if w_router is None else F32)
    if w_router is None:
        return pl.pallas_call(_norm_mod_body, grid=(T // tm,), in_specs=in_specs, out_specs=o_spec, out_shape=o_shape,
                              compiler_params=_params(("parallel",)), name="norm_mod")(*args)
    wr = jnp.pad(w_router, ((0, 0), (0, LANES - w_router.shape[1])))
    return pl.pallas_call(
        _norm_mod_router_body, grid=(T // tm,),
        in_specs=in_specs + [pl.BlockSpec((D, LANES), lambda i: (0, 0))],
        out_specs=[o_spec, pl.BlockSpec((tm, LANES), lambda i: (i, 0))],
        out_shape=[o_shape, jax.ShapeDtypeStruct((T, LANES), F32)],
        compiler_params=_params(("parallel",)), name="norm_mod_router")(*args, wr)


def _mm_body(*refs, n_extra, n_out, epi):
    a_ref, b_ref = refs[0], refs[1]
    extra = refs[2:2 + n_extra]
    outs = refs[2 + n_extra:2 + n_extra + n_out]
    acc_ref = refs[-1]
    k = pl.program_id(2)

    @pl.when(k == 0)
    def _():
        acc_ref[...] = jnp.zeros_like(acc_ref)

    acc_ref[...] += jnp.dot(a_ref[...], b_ref[...], preferred_element_type=F32)

    @pl.when(k == pl.num_programs(2) - 1)
    def _():
        epi(acc_ref[...], extra, outs)


def matmul(a, b, *, epi, out_shapes, out_specs, tm, tn, tk=None, extra=(), extra_specs=(),
           n_row_blocks=None, row_block_off=0, name="matmul"):
    M, K = a.shape
    N = b.shape[1]
    if tk is None:
        tk = K if 2 * K * tn * 2 <= MM_WEIGHT_BYTES else _tile(K, 1024, LANES)
    nrb = M // tm if n_row_blocks is None else n_row_blocks
    grid = (nrb, N // tn, K // tk)
    return pl.pallas_call(
        functools.partial(_mm_body, n_extra=len(extra), n_out=len(out_shapes), epi=epi),
        grid=grid,
        in_specs=[pl.BlockSpec((tm, tk), lambda i, j, k: (i + row_block_off, k)),
                  pl.BlockSpec((tk, tn), lambda i, j, k: (k, j))] + list(extra_specs),
        out_specs=list(out_specs), out_shape=list(out_shapes),
        scratch_shapes=[pltpu.VMEM((tm, tn), F32)],
        compiler_params=_params(("parallel", "parallel", "arbitrary")), name=name)(a, b, *extra)


def _epi_plain(acc, extra, outs):
    outs[0][...] = acc.astype(outs[0].dtype)


def matmul_plain(a, b, out_dtype=F32, name="matmul"):
    M, N = a.shape[0], b.shape[1]
    tn = _tile(N, 1024, 2 * LANES)
    tm = _tile(M, MM_ACC_ELEMS // tn, 2 * SUBLANES)
    return matmul(a, b, epi=_epi_plain, out_shapes=[jax.ShapeDtypeStruct((M, N), out_dtype)],
                  out_specs=[pl.BlockSpec((tm, tn), lambda i, j, k: (i, j))], tm=tm, tn=tn, name=name)[0]


def _rope_mix(t, tab):
    lane = lax.broadcasted_iota(jnp.int32, t.shape, 1)
    ss = jnp.sum(jnp.where(lane < ROPE, t * t, 0.0), axis=-1, keepdims=True)
    m = t * tab
    return (m + pltpu.roll(m, ROPE, axis=1)) * lax.rsqrt(ss * (1.0 / ROPE) + RMS_EPS)


def _epi_mq(acc, extra, outs):
    outs[0][...] = _rms(acc, extra[0][...]).astype(outs[0].dtype)


def _epi_mkv(acc, extra, outs):
    g_ref, tab_ref = extra
    outs[0][...] = _rms(acc[:, :KV_LORA], g_ref[...]).astype(outs[0].dtype)
    outs[1][...] = _rope_mix(acc[:, KV_LORA:], tab_ref[...])[:, :ROPE]


def _epi_qup(acc, extra, outs):
    g_ref, tab_ref = extra
    o = outs[0]
    for h in range(o.shape[0]):
        a = acc[:, h * QH:(h + 1) * QH]
        o[h, :, :NOPE] = (_rms(a[:, :NOPE], g_ref[...]) * Q_SCALE).astype(o.dtype)
        o[h, :, NOPE:] = (_rope_mix(a[:, NOPE:], tab_ref[...])[:, :ROPE] * Q_SCALE).astype(o.dtype)


def _epi_kvup(acc, extra, outs):
    g_ref, kr_ref = extra
    ok, ov = outs
    kr = kr_ref[...].astype(ok.dtype)
    for h in range(ok.shape[0]):
        a = acc[:, h * (NOPE + V_HEAD):(h + 1) * (NOPE + V_HEAD)]
        ok[h, :, :NOPE] = _rms(a[:, :NOPE], g_ref[...]).astype(ok.dtype)
        ok[h, :, NOPE:] = kr
        ov[h] = a[:, NOPE:].astype(ov.dtype)


def _epi_res(acc, extra, outs):
    res_ref, gate_ref = extra
    outs[0][...] = res_ref[...] + gate_ref[...] * acc


def mla_q_up(cq, w, qn_g, tab, B, Lx, row_off):
    tm = _tile(Lx, 512, 2 * SUBLANES)
    nt = Lx // tm
    hg = MLA_HEAD_GROUP
    return matmul(
        cq, w, epi=_epi_qup, tm=tm, tn=hg * QH, tk=Q_LORA,
        extra=(qn_g.reshape(1, NOPE), tab),
        extra_specs=(pl.BlockSpec((1, NOPE), lambda i, j, k: (0, 0)),
                     pl.BlockSpec((tm, 2 * ROPE), lambda i, j, k: (i % nt, 0))),
        out_shapes=[jax.ShapeDtypeStruct((B, N_HEADS_M, Lx, KD), BF16)],
        out_specs=[pl.BlockSpec((None, hg, tm, KD), lambda i, j, k: (i // nt, j, i % nt, 0))],
        n_row_blocks=B * nt, row_block_off=row_off // tm, name="mla_q_up")[0]


def mla_kv_up(ckv, w, kn_g, kr, B, Lx, row_off):
    tm = _tile(Lx, 512, 2 * SUBLANES)
    nt = Lx // tm
    rbo = row_off // tm
    hg = MLA_HEAD_GROUP
    return matmul(
        ckv, w, epi=_epi_kvup, tm=tm, tn=hg * (NOPE + V_HEAD), tk=KV_LORA,
        extra=(kn_g.reshape(1, NOPE), kr),
        extra_specs=(pl.BlockSpec((1, NOPE), lambda i, j, k: (0, 0)),
                     pl.BlockSpec((tm, ROPE), lambda i, j, k: (i + rbo, 0))),
        out_shapes=[jax.ShapeDtypeStruct((B, N_HEADS_M, Lx, KD), BF16),
                    jax.ShapeDtypeStruct((B, N_HEADS_M, Lx, V_HEAD), BF16)],
        out_specs=[pl.BlockSpec((None, hg, tm, KD), lambda i, j, k: (i // nt, j, i % nt, 0)),
                   pl.BlockSpec((None, hg, tm, V_HEAD), lambda i, j, k: (i // nt, j, i % nt, 0))],
        n_row_blocks=B * nt, row_block_off=rbo, name="mla_kv_up")


def _outproj_body(a0_ref, a1_ref, a2_ref, b_ref, *rest, offs, na):
    res_refs, (gate_ref, o_ref) = rest[:-2], rest[-2:]
    acc = None
    for a_ref, lo in zip((a0_ref, a1_ref, a2_ref), offs):
        d = jnp.dot(a_ref[...], b_ref[pl.ds(lo, a_ref.shape[1]), :], preferred_element_type=F32)
        acc = d if acc is None else acc + d
    if len(res_refs) == 2:
        res = jnp.where(pl.program_id(0) < na, res_refs[0][...], res_refs[1][...])
    else:
        res = res_refs[0][...]
    o_ref[...] = res + gate_ref[...] * acc


def out_proj(parts, w, res, gate, rows, rows_per_mod):
    K, N = w.shape
    R = gate.shape[0]
    tn = _tile(N, 1024, 2 * LANES)
    tm = _tile(math.gcd(rows, rows_per_mod), MM_ACC_ELEMS // tn, 2 * SUBLANES)
    offs, lo = [], 0
    for p in parts:
        offs.append(lo)
        lo += p.shape[1]
    assert lo == K
    o_spec = pl.BlockSpec((tm, tn), lambda i, j: (i, j))
    if isinstance(res, tuple):
        na = res[0].shape[0] // tm
        assert res[0].shape[0] % tm == 0 and res[1].shape[0] % tm == 0
        res_args = list(res)
        res_specs = [pl.BlockSpec((tm, tn), lambda i, j: (jnp.minimum(i, na - 1), j)),
                     pl.BlockSpec((tm, tn), lambda i, j: (jnp.maximum(i - na, 0), j))]
    else:
        na, res_args, res_specs = 0, [res], [o_spec]
    return pl.pallas_call(
        functools.partial(_outproj_body, offs=tuple(offs), na=na),
        grid=(rows // tm, N // tn),
        in_specs=[pl.BlockSpec((tm, p.shape[1]), lambda i, j: (i, 0)) for p in parts] + [
            pl.BlockSpec((K, tn), lambda i, j: (0, j))] + res_specs + [
            pl.BlockSpec((None, 1, tn), lambda i, j: (jnp.minimum(i * tm // rows_per_mod, R - 1), 0, j))],
        out_specs=o_spec,
        out_shape=jax.ShapeDtypeStruct((rows, N), F32),
        compiler_params=_params(("parallel", "parallel")),
        name="out_proj")(*parts, w, *res_args, gate.reshape(R, 1, N))


def _seq_row_block(b, i, t, L, Lc, nlat):
    nc = Lc // t
    return jnp.where(i < nc, (nlat + b * Lc) // t + i, (b * L) // t + (i - nc))


def _conv_body(u_ref, up_ref, un_ref, dw_ref, db_ref, g_ref, b_ref, o_ref, buf_ref, sh_ref, *, tl, rc, nc, i0):
    i = pl.program_id(1) + i0
    nt = pl.num_programs(1) + i0
    first_tile = (i == 0) | (i == nc)
    last_tile = (i == nc - 1) | (i == nt - 1)

    def glu(u):
        return u[:, :C_CONV] * jax.nn.sigmoid(u[:, C_CONV:])

    buf_ref[pl.ds(0, CONV_HALO), :] = jnp.where(first_tile, 0.0, glu(up_ref[...]))
    buf_ref[pl.ds(CONV_HALO, tl), :] = glu(u_ref[...])
    buf_ref[pl.ds(CONV_HALO + tl, CONV_HALO), :] = jnp.where(last_tile, 0.0, glu(un_ref[...]))
    nsh = tl + 2 * CONV_HALO - SUBLANES
    for s in range(1, SUBLANES):
        sh_ref[s - 1] = buf_ref[pl.ds(s, nsh), :]

    first = CONV_HALO - CONV_W // 2

    def chunk(c, carry):
        r0 = pl.multiple_of(c * rc, rc)
        acc = jnp.broadcast_to(db_ref[...], (rc, C_CONV))
        for k in range(CONV_W):
            q, s = divmod(first + k, SUBLANES)
            start = pl.multiple_of(r0 + q * SUBLANES, SUBLANES)
            win = buf_ref[pl.ds(start, rc), :] if s == 0 else sh_ref[s - 1, pl.ds(start, rc), :]
            acc = acc + dw_ref[pl.ds(k, 1), :] * win
        mu = jnp.mean(acc, axis=-1, keepdims=True)
        d = acc - mu
        var = jnp.mean(d * d, axis=-1, keepdims=True)
        y = d * lax.rsqrt(var + LN_EPS) * g_ref[...] + b_ref[...]
        o_ref[pl.ds(r0, rc), :] = _silu(y).astype(o_ref.dtype)
        return carry

    lax.fori_loop(0, tl // rc, chunk, 0)


def conformer_conv(u, dw, db, ln_g, ln_b, B, L, Lc, with_ctx):
    T = u.shape[0]
    nlat = B * L
    tl = _tile(math.gcd(L, Lc), 256, CONV_HALO)
    rc = _tile(tl, 32, SUBLANES)
    nc = Lc // tl
    i0 = 0 if with_ctx else nc
    hb = tl // CONV_HALO
    nhalo = T // CONV_HALO
    row = lambda b, i: _seq_row_block(b, i + i0, tl, L, Lc, nlat)
    body = functools.partial(_conv_body, tl=tl, rc=rc, nc=nc, i0=i0)
    vec = lambda a: a.reshape(1, C_CONV)
    vspec = pl.BlockSpec((1, C_CONV), lambda b, i: (0, 0))
    return pl.pallas_call(
        body,
        grid=(B, (L + Lc) // tl - i0),
        in_specs=[pl.BlockSpec((tl, 2 * C_CONV), lambda b, i: (row(b, i), 0)),
                  pl.BlockSpec((CONV_HALO, 2 * C_CONV), lambda b, i: (jnp.maximum(row(b, i) * hb - 1, 0), 0)),
                  pl.BlockSpec((CONV_HALO, 2 * C_CONV), lambda b, i: (jnp.minimum((row(b, i) + 1) * hb, nhalo - 1), 0)),
                  pl.BlockSpec((CONV_W, C_CONV), lambda b, i: (0, 0)),
                  vspec, vspec, vspec],
        out_specs=pl.BlockSpec((tl, C_CONV), lambda b, i: (row(b, i), 0)),
        out_shape=jax.ShapeDtypeStruct((T if with_ctx else nlat, C_CONV), BF16),
        scratch_shapes=[pltpu.VMEM((tl + 2 * CONV_HALO, C_CONV), F32),
                        pltpu.VMEM((SUBLANES - 1, tl + 2 * CONV_HALO - SUBLANES, C_CONV), F32)],
        compiler_params=_params(("parallel", "arbitrary")),
        name="conformer_conv",
    )(u, u, u, dw, vec(db), vec(ln_g), vec(ln_b))


def _softplus(z):
    return jnp.maximum(z, 0.0) + jnp.log(1.0 + jnp.exp(-jnp.abs(z)))


def _feat_body(pr_ref, prv_ref, nxt_ref, mup_ref, mun_ref, w0_ref, w2_ref, a0_ref, a2_ref, kkp_ref, ka_ref,
               rk_ref, g2_ref, bd_ref, p1f_ref, p1b_ref, p2f_ref, p2b_ref, p3_ref, bonus_ref, gate_ref, buf_ref,
               *, tm, nc):
    i = pl.program_id(1)
    nt = pl.num_programs(1)
    first_tile = (i == 0) | (i == nc)
    last_tile = (i == nc - 1) | (i == nt - 1)
    buf_ref[pl.ds(SUBLANES, tm), :] = pr_ref[...]
    buf_ref[pl.ds(0, SUBLANES), :] = jnp.where(first_tile, 0.0, prv_ref[...])
    buf_ref[pl.ds(SUBLANES + tm, SUBLANES), :] = jnp.where(last_tile, 0.0, nxt_ref[...])

    def shifted(lo, hi):
        p = buf_ref[pl.ds(SUBLANES, tm), lo:hi]
        pv = buf_ref[pl.ds(SUBLANES - 1, tm), lo:hi]
        nx = buf_ref[pl.ds(SUBLANES + 1, tm), lo:hi]
        return p + mup_ref[:, lo:hi] * (pv - p) + mun_ref[:, lo:hi] * (nx - p)

    def head_sum(x):
        return jnp.concatenate(
            [jnp.dot(x[:, c * LANES:(c + 1) * LANES], bd_ref[...], precision=HP, preferred_element_type=F32)
             for c in range(C_R // LANES)], axis=-1)

    r = shifted(RC_R, RC_K)
    k = shifted(RC_K, RC_V)
    v = shifted(RC_V, RC_W)
    wd = jnp.tanh(shifted(RC_W, RC_A))
    ad = shifted(RC_A, RC_G)
    rg = shifted(RC_G, RCOLS)
    wl = -_softplus(-(w0_ref[...] + jnp.dot(wd, w2_ref[...], precision=HP, preferred_element_type=F32))) - 0.5
    dec = jnp.exp(-jnp.exp(wl))
    a = jax.nn.sigmoid(a0_ref[...] + jnp.dot(ad, a2_ref[...], precision=HP, preferred_element_type=F32))
    kkv = k * kkp_ref[...]
    kk = kkv / jnp.maximum(jnp.sqrt(head_sum(kkv * kkv)), 1e-12)
    gate_ref[...] = jnp.dot(jax.nn.sigmoid(rg), g2_ref[...], precision=HP, preferred_element_type=F32)

    lane = lax.broadcasted_iota(jnp.int32, (tm, LANES), 1)
    low = lane < HEAD_R

    def pack(o_ref, x, y):
        heads = []
        for c in range(C_R // LANES):
            xa = x[:, c * LANES:(c + 1) * LANES]
            ya = y[:, c * LANES:(c + 1) * LANES]
            heads.append(jnp.where(low, xa, pltpu.roll(ya, HEAD_R, axis=1)))
            heads.append(jnp.where(low, pltpu.roll(xa, HEAD_R, axis=1), ya))
        o_ref[...] = pltpu.einshape("htl->thl", jnp.stack(heads, axis=0))

    bonus = jnp.zeros((tm, C_R), F32)
    for d, (o1, o2) in enumerate(((p1f_ref, p2f_ref), (p1b_ref, p2b_ref))):
        a_d = a[:, d * C_R:(d + 1) * C_R]
        k_d = k * (1.0 + (a_d - 1.0) * ka_ref[...])
        pack(o1, dec[:, d * C_R:(d + 1) * C_R], kk * a_d)
        pack(o2, k_d, kk)
        bonus = bonus + head_sum(r * k_d * rk_ref[...]) * v
    pack(p3_ref, v, r)
    bonus_ref[...] = bonus


def rwkv_features(pr, rp, B, L, Lc):
    T = pr.shape[0]
    nlat = B * L
    Lt = L + Lc
    tm = _tile(math.gcd(L, Lc), 256, SUBLANES)
    nc = Lc // tm
    nrow8 = T // SUBLANES
    hb = tm // SUBLANES
    row = lambda b, i: _seq_row_block(b, i, tm, L, Lc, nlat)
    full = lambda a: pl.BlockSpec(a.shape, lambda b, i: (0,) * a.ndim)
    consts = (rp["mu_prev"], rp["mu_next"], rp["w0"], rp["w2"], rp["a0"], rp["a2"], rp["kk"], rp["ka"], rp["rk"],
              rp["g2"], rp["bd"])
    pk_shape = jax.ShapeDtypeStruct((Lt, B * N_HEADS_R, 2 * HEAD_R), F32)
    pk_spec = pl.BlockSpec((tm, N_HEADS_R, 2 * HEAD_R), lambda b, i: (i, b, 0))
    tok_shape = jax.ShapeDtypeStruct((T, C_R), F32)
    tok_spec = pl.BlockSpec((tm, C_R), lambda b, i: (row(b, i), 0))
    return pl.pallas_call(
        functools.partial(_feat_body, tm=tm, nc=nc),
        grid=(B, Lt // tm),
        in_specs=[pl.BlockSpec((tm, RCOLS), lambda b, i: (row(b, i), 0)),
                  pl.BlockSpec((SUBLANES, RCOLS), lambda b, i: (jnp.maximum(row(b, i) * hb - 1, 0), 0)),
                  pl.BlockSpec((SUBLANES, RCOLS), lambda b, i: (jnp.minimum((row(b, i) + 1) * hb, nrow8 - 1), 0))]
                 + [full(a) for a in consts],
        out_specs=[pk_spec] * 5 + [tok_spec] * 2,
        out_shape=[pk_shape] * 5 + [tok_shape] * 2,
        scratch_shapes=[pltpu.VMEM((tm + 2 * SUBLANES, RCOLS), F32)],
        compiler_params=_params(("parallel", "arbitrary")),
        name="rwkv_features",
    )(pr, pr, pr, *consts)


def _fold8(parts):
    sub = lax.broadcasted_iota(jnp.int32, parts[0].shape, 0)
    d = 1
    while len(parts) > 1:
        m = (sub // d) % 2 == 0
        parts = [jnp.where(m, a, b) + pltpu.roll(jnp.where(m, b, a), d, axis=0)
                 for a, b in zip(parts[0::2], parts[1::2])]
        d *= 2
    return parts[0]


def _scan_body(p1f_ref, p1b_ref, p2f_ref, p2b_ref, p3f_ref, p3b_ref, yf_ref, yb_ref, s_ref, op_ref, ybuf_ref,
               *, tc, n, nch):
    @pl.when(pl.program_id(0) == 0)
    def _():
        s_ref[...] = jnp.zeros_like(s_ref)

    def prep(t, g_prev):
        tb = tc - 1 - t
        zt = [jnp.concatenate([f_ref[t], b_ref[tb]], axis=0).T
              for f_ref, b_ref in ((p1f_ref, p1b_ref), (p2f_ref, p2b_ref), (p3f_ref, p3b_ref))]
        g = g_prev * zt[0][:n]
        inv = 1.0 / g
        op_ref[t, 0] = zt[1][n:] * g_prev
        op_ref[t, 1] = zt[0][n:] * inv
        op_ref[t, 2] = zt[1][:n] * inv
        op_ref[t, 3] = zt[2][n:] * g
        op_ref[t, 4] = zt[2][:n]
        return g

    g_last = lax.fori_loop(0, tc, prep, jnp.ones((n, LANES), F32), unroll=4)
    nv = n // SUBLANES

    def colsum(x):
        return jnp.sum(x.reshape(nv, SUBLANES, x.shape[-1]), axis=0)

    def step(t, carry):
        kkg = op_ref[t, 0]
        kkai = op_ref[t, 1]
        ki = op_ref[t, 2]
        rg = op_ref[t, 3]

        def rows(ib, c2):
            base = pl.multiple_of(ib * SUBLANES, SUBLANES)
            sa8 = _fold8([colsum(s_ref[base + ii] * kkg) for ii in range(SUBLANES)])
            v8 = op_ref[t, 4, pl.ds(base, SUBLANES), :]
            py = []
            for ii in range(SUBLANES):
                s_new = s_ref[base + ii] - sa8[ii:ii + 1] * kkai + v8[ii:ii + 1] * ki
                s_ref[base + ii] = s_new
                py.append(colsum(s_new * rg))
            ybuf_ref[t, pl.ds(base, SUBLANES), :] = _fold8(py)
            return c2

        lax.fori_loop(0, nv, rows, 0, unroll=True)
        return carry

    lax.fori_loop(0, tc, step, 0)

    def rescale(i, carry):
        s_ref[i] = s_ref[i] * g_last
        return carry

    lax.fori_loop(0, n, rescale, 0, unroll=4)

    def fin(t, carry):
        y = ybuf_ref[t]
        yt = jnp.concatenate([y, y], axis=0).T
        yf_ref[t] = yt[:nch, :n]
        yb_ref[tc - 1 - t] = yt[nch:, :n]
        return carry

    lax.fori_loop(0, tc, fin, 0, unroll=4)


def wkv_scan(p1f, p1b, p2f, p2b, p3, Lc):
    Lt, nch, _ = p1f.shape
    n = HEAD_R
    assert 2 * nch == LANES and 2 * n == LANES
    tc = _tile(math.gcd(Lc, Lt - Lc), 32, 1)
    ncb, nb = Lc // tc, Lt // tc
    fwd = lambda g: (g, 0, 0)
    bwd = lambda g: (jnp.where(g < ncb, ncb - 1 - g, nb - 1 - (g - ncb)), 0, 0)
    ispec = lambda m: pl.BlockSpec((tc, nch, 2 * n), m)
    ospec = lambda m: pl.BlockSpec((tc, nch, n), m)
    y_shape = jax.ShapeDtypeStruct((Lt, nch, n), F32)
    return pl.pallas_call(
        functools.partial(_scan_body, tc=tc, n=n, nch=nch),
        grid=(nb,),
        in_specs=[ispec(fwd), ispec(bwd), ispec(fwd), ispec(bwd), ispec(fwd), ispec(bwd)],
        out_specs=[ospec(fwd), ospec(bwd)],
        out_shape=[y_shape, y_shape],
        scratch_shapes=[pltpu.VMEM((n, n, LANES), F32), pltpu.VMEM((tc, 5, n, LANES), F32),
                        pltpu.VMEM((tc, n, LANES), F32)],
        compiler_params=_params(("arbitrary",)),
        name="wkv_scan",
    )(p1f, p1b, p2f, p2b, p3, p3)


def _post_body(yf_ref, yb_ref, bonus_ref, gate_ref, g_ref, b_ref, o_ref):
    yh = pltpu.einshape("thl->htl", yf_ref[...] + yb_ref[...])
    cols = []
    for h in range(N_HEADS_R):
        y = yh[h]
        mu = jnp.mean(y, axis=-1, keepdims=True)
        d = y - mu
        var = jnp.mean(d * d, axis=-1, keepdims=True)
        cols.append(d * lax.rsqrt(var + GN_EPS))
    yn = jnp.concatenate(cols, axis=-1) * g_ref[...] + b_ref[...]
    o_ref[...] = ((yn + bonus_ref[...]) * gate_ref[...]).astype(o_ref.dtype)


def rwkv_output(yf, yb, bonus, gate, ln_g, ln_b, B, L, Lc, with_ctx):
    T = bonus.shape[0]
    nlat = B * L
    tm = _tile(math.gcd(L, Lc), 256, 2 * SUBLANES)
    i0 = 0 if with_ctx else Lc // tm
    row = lambda b, i: _seq_row_block(b, i + i0, tm, L, Lc, nlat)
    y_spec = pl.BlockSpec((tm, N_HEADS_R, HEAD_R), lambda b, i: (i + i0, b, 0))
    tok_spec = pl.BlockSpec((tm, C_R), lambda b, i: (row(b, i), 0))
    vspec = pl.BlockSpec((1, C_R), lambda b, i: (0, 0))
    return pl.pallas_call(
        _post_body,
        grid=(B, (L + Lc) // tm - i0),
        in_specs=[y_spec, y_spec, tok_spec, tok_spec, vspec, vspec],
        out_specs=tok_spec,
        out_shape=jax.ShapeDtypeStruct((T if with_ctx else nlat, C_R), BF16),
        compiler_params=_params(("parallel", "parallel")),
        name="rwkv_output",
    )(yf, yb, bonus, gate, ln_g.reshape(1, C_R), ln_b.reshape(1, C_R))


def _attn_body(*refs, n_kv, hq):
    kv = refs[1:1 + 2 * n_kv]
    o_ref = refs[-1]
    for a in range(refs[0].shape[0] // hq):
        q = refs[0][pl.ds(a * hq, hq), :]
        s = [lax.dot_general(q, kv[2 * j][...], (((1,), (1,)), ((), ())), preferred_element_type=F32)
             for j in range(n_kv)]
        m = functools.reduce(jnp.maximum, [jnp.max(x, axis=-1, keepdims=True) for x in s])
        p = [jnp.exp2(x - m) for x in s]
        l = functools.reduce(jnp.add, [jnp.sum(x, axis=-1, keepdims=True) for x in p])
        o = functools.reduce(jnp.add, [jnp.dot(p[j].astype(BF16), kv[2 * j + 1][...], preferred_element_type=F32)
                                       for j in range(n_kv)])
        o_ref[pl.ds(a * hq, hq), :] = (o / l).astype(o_ref.dtype)


def mla_attention(q, kvs, rows_total, row_off, prev=None):
    B, H, Lq, dk = q.shape
    dv = kvs[0][1].shape[3]
    hq = _tile(Lq, ATTN_SUBTILE, 2 * SUBLANES)
    tq = _tile(Lq, 2 * hq, hq)
    nq = Lq // tq
    rbo = row_off // tq
    in_specs = [pl.BlockSpec((None, None, tq, dk), lambda b, h, i: (b, h, i, 0))]
    args = [q]
    for k, v in kvs:
        in_specs += [pl.BlockSpec((None, None, k.shape[2], dk), lambda b, h, i: (b, h, 0, 0)),
                     pl.BlockSpec((None, None, v.shape[2], dv), lambda b, h, i: (b, h, 0, 0))]
        args += [k, v]
    body = functools.partial(_attn_body, n_kv=len(kvs), hq=hq)
    aliases = {}
    if prev is not None:
        in_specs.append(pl.BlockSpec(memory_space=pl.ANY))
        args.append(prev)
        aliases = {len(args) - 1: 0}
        body = lambda *refs: _attn_body(*refs[:-2], refs[-1], n_kv=len(kvs), hq=hq)
    return pl.pallas_call(
        body,
        grid=(B, H, nq),
        in_specs=in_specs,
        out_specs=pl.BlockSpec((tq, dv), lambda b, h, i: (rbo + b * nq + i, h)),
        out_shape=jax.ShapeDtypeStruct((rows_total, H * dv), BF16),
        input_output_aliases=aliases,
        compiler_params=_params(("parallel", "parallel", "arbitrary")),
        name="mla_attention",
    )(*args)


def _moe_body(be_ref, nu_ref, x_ref, w1_ref, w3_ref, w2_ref, o_ref):
    blk = pl.program_id(0)
    f = pl.program_id(1)

    @pl.when(f == 0)
    def _():
        o_ref[...] = jnp.zeros_like(o_ref)

    @pl.when(blk < nu_ref[0])
    def _():
        x = x_ref[...]
        h1 = jnp.dot(x, w1_ref[...], preferred_element_type=F32)
        h3 = jnp.dot(x, w3_ref[...], preferred_element_type=F32)
        hb = (_silu(h1) * h3).astype(BF16)
        o_ref[...] += jnp.dot(hb, w2_ref[...], preferred_element_type=F32)


def moe_experts(xs, block_e, n_used, w1, w3, w2):
    P, D = xs.shape
    Fd = w1.shape[2]
    tf = _tile(Fd, 512, LANES)
    nf = Fd // tf
    nb = P // MOE_ROWS

    def fidx(i, f, nu):
        return jnp.where(i < nu[0], f, nf - 1)

    grid_spec = pltpu.PrefetchScalarGridSpec(
        num_scalar_prefetch=2,
        grid=(nb, nf),
        in_specs=[pl.BlockSpec((MOE_ROWS, D), lambda i, f, be, nu: (i, 0)),
                  pl.BlockSpec((None, D, tf), lambda i, f, be, nu: (be[i], 0, fidx(i, f, nu))),
                  pl.BlockSpec((None, D, tf), lambda i, f, be, nu: (be[i], 0, fidx(i, f, nu))),
                  pl.BlockSpec((None, tf, D), lambda i, f, be, nu: (be[i], fidx(i, f, nu), 0))],
        out_specs=pl.BlockSpec((MOE_ROWS, D), lambda i, f, be, nu: (i, 0)),
    )
    return pl.pallas_call(
        _moe_body,
        grid_spec=grid_spec,
        out_shape=jax.ShapeDtypeStruct((P, D), F32),
        compiler_params=_params(("arbitrary", "arbitrary")),
        name="moe_experts",
    )(block_e, n_used, xs, w1, w3, w2)


def _gather_body(*refs, n_src, tm, combine):
    idx_refs = refs[:n_src]
    src_ref = refs[n_src]
    rest = refs[n_src + 1:]
    if combine:
        w_ref, base_ref, gate_ref, o_ref, buf_ref, sem_ref = rest
    else:
        o_ref, buf_ref, sem_ref = rest
    i = pl.program_id(0)
    nsteps = pl.num_programs(0)
    slot = i % 2

    def row_copy(step, r, j, sl):
        row = idx_refs[j][step * tm + r]
        return pltpu.make_async_copy(src_ref.at[pl.ds(row, 1), :], buf_ref.at[sl, j, pl.ds(r, 1), :], sem_ref.at[sl])

    def start_all(step, sl):
        def body(r, c):
            for j in range(n_src):
                row_copy(step, r, j, sl).start()
            return c
        lax.fori_loop(0, tm, body, 0, unroll=8)

    def wait_all(step, sl):
        def body(r, c):
            for j in range(n_src):
                row_copy(step, r, j, sl).wait()
            return c
        lax.fori_loop(0, tm, body, 0, unroll=8)

    @pl.when(i == 0)
    def _():
        start_all(0, 0)

    @pl.when(i + 1 < nsteps)
    def _():
        start_all(i + 1, 1 - slot)

    wait_all(i, slot)
    if combine:
        acc = buf_ref[slot, 0] * w_ref[:, 0:1]
        for j in range(1, n_src):
            acc = acc + buf_ref[slot, j] * w_ref[:, j:j + 1]
        o_ref[...] = base_ref[...] + gate_ref[...] * acc
    else:
        o_ref[...] = buf_ref[slot, 0].astype(o_ref.dtype)


def gather_rows(src, idx, out_dtype):
    R = idx.shape[0]
    D = src.shape[1]
    tm = _tile(R, GATHER_ROWS, 2 * SUBLANES)
    grid_spec = pltpu.PrefetchScalarGridSpec(
        num_scalar_prefetch=1, grid=(R // tm,),
        in_specs=[pl.BlockSpec(memory_space=pl.ANY)],
        out_specs=pl.BlockSpec((tm, D), lambda i, ix: (i, 0)),
        scratch_shapes=[pltpu.VMEM((2, 1, tm, D), F32), pltpu.SemaphoreType.DMA((2,))])
    return pl.pallas_call(
        functools.partial(_gather_body, n_src=1, tm=tm, combine=False), grid_spec=grid_spec,
        out_shape=jax.ShapeDtypeStruct((R, D), out_dtype),
        compiler_params=_params(("arbitrary",)), name="moe_dispatch")(idx, src)


def combine_rows(src, idxs, w, base, gate, rows_per_mod):
    T, D = base.shape
    n_src = len(idxs)
    Rg = gate.shape[0]
    tm = _tile(math.gcd(T, rows_per_mod), GATHER_ROWS // n_src, SUBLANES)
    grid_spec = pltpu.PrefetchScalarGridSpec(
        num_scalar_prefetch=n_src, grid=(T // tm,),
        in_specs=[pl.BlockSpec(memory_space=pl.ANY),
                  pl.BlockSpec((tm, n_src), lambda i, *ix: (i, 0)),
                  pl.BlockSpec((tm, D), lambda i, *ix: (i, 0)),
                  pl.BlockSpec((None, 1, D), lambda i, *ix: (jnp.minimum(i * tm // rows_per_mod, Rg - 1), 0, 0))],
        out_specs=pl.BlockSpec((tm, D), lambda i, *ix: (i, 0)),
        scratch_shapes=[pltpu.VMEM((2, n_src, tm, D), F32), pltpu.SemaphoreType.DMA((2,))])
    return pl.pallas_call(
        functools.partial(_gather_body, n_src=n_src, tm=tm, combine=True), grid_spec=grid_spec,
        out_shape=jax.ShapeDtypeStruct((T, D), F32),
        compiler_params=_params(("arbitrary",)), name="moe_combine")(*idxs, src, w, base, gate.reshape(Rg, 1, D))


def _top2(x):
    pos = lax.broadcasted_iota(jnp.int32, x.shape, x.ndim - 1)
    i1 = jnp.argmax(x, -1)
    m1 = jnp.max(x, -1)
    x2 = jnp.where(pos == i1[..., None], -jnp.inf, x)
    return jnp.stack([m1, jnp.max(x2, -1)], -1), jnp.stack([i1, jnp.argmax(x2, -1)], -1)


def _route(logits, router_bias):
    scores = jax.nn.sigmoid(logits)
    biased = (scores + router_bias.astype(F32)).reshape(-1, N_GROUPS, EXPERTS_PER_GROUP)
    group_score = jnp.sum(_top2(biased)[0], -1)
    chosen = jnp.argmax(group_score, -1)
    in_group = jnp.arange(N_GROUPS)[None, :] == chosen[:, None]
    masked = jnp.where(in_group[:, :, None], biased, -jnp.inf).reshape(-1, N_EXPERTS)
    _, idx = _top2(masked)
    w = jnp.take_along_axis(scores, idx, -1)
    return idx, w / jnp.sum(w, -1, keepdims=True)


def moe_ffn(tok, logits, router_bias, w1, w3, w2, base, gate, rows_per_mod):
    T, D = tok.shape
    idx, wts = _route(logits, router_bias)
    M = T * TOP_K
    flat_e = idx.reshape(-1).astype(jnp.int32)
    onehot = (flat_e[:, None] == jnp.arange(N_EXPERTS, dtype=jnp.int32)[None, :]).astype(jnp.int32)
    csum = jnp.cumsum(onehot, axis=0)
    rank = jnp.take_along_axis(csum, flat_e[:, None], axis=1)[:, 0] - 1
    counts = csum[-1]
    pcounts = (counts + MOE_ROWS - 1) // MOE_ROWS * MOE_ROWS
    pends = jnp.cumsum(pcounts)
    pstarts = pends - pcounts
    dest = pstarts[flat_e] + rank
    n_blocks = -(-M // MOE_ROWS) + N_EXPERTS
    P = n_blocks * MOE_ROWS
    flat_t = jnp.repeat(jnp.arange(T, dtype=jnp.int32), TOP_K)
    row_tok = jnp.zeros((P,), jnp.int32).at[dest].set(flat_t)
    n_used = (pends[-1] // MOE_ROWS).astype(jnp.int32)
    blk_start = jnp.arange(n_blocks, dtype=jnp.int32) * MOE_ROWS
    block_e = jnp.clip(jnp.searchsorted(pends, blk_start, side="right"), 0, N_EXPERTS - 1).astype(jnp.int32)
    last_e = block_e[jnp.maximum(n_used - 1, 0)]
    block_e = jnp.where(jnp.arange(n_blocks) < n_used, block_e, last_e)
    xs = gather_rows(tok, row_tok, BF16)
    yb = moe_experts(xs, block_e, n_used.reshape(1), w1, w3, w2)
    dest = dest.reshape(T, TOP_K)
    return combine_rows(yb, [dest[:, j] for j in range(TOP_K)], wts, base, gate, rows_per_mod)


def _rope_tables(rows):
    half = ROPE // 2
    inv_freq = ROPE_BASE ** (-jnp.arange(0, half, 2, dtype=F32) / half)
    r = jnp.repeat(jnp.arange(rows, dtype=F32), GRID_W)
    cl = jnp.tile(jnp.arange(GRID_W, dtype=F32), rows)
    ar = r[:, None] * inv_freq
    ac = cl[:, None] * inv_freq
    ang = jnp.concatenate([ar, ar, ac, ac], -1)
    return jnp.cos(ang), jnp.sin(ang)


def _rot_cols(w):
    q = ROPE // 4
    return jnp.concatenate([-w[..., q:2 * q], w[..., :q], -w[..., 3 * q:], w[..., 2 * q:3 * q]], -1)


def _rot_perm(g):
    q = ROPE // 4
    return jnp.concatenate([g[q:2 * q], g[:q], g[3 * q:], g[2 * q:3 * q]], -1)


def _rope_tab(g, cos, sin):
    if cos is None:
        return jnp.concatenate([g, jnp.zeros_like(g)])[None, :]
    return jnp.concatenate([g[None, :] * cos, _rot_perm(g)[None, :] * sin], -1)


def _blockdiag2(w):
    z = jnp.zeros_like(w[0])
    return jnp.concatenate([jnp.concatenate([w[0], z], 1), jnp.concatenate([z, w[1]], 1)], 0)


def kernel(x, c, ctx, c_ctx, w_mod, b_mod, norm1_g, norm2_g, w_in, w_out, conv_dw, conv_b, conv_ln_g, conv_ln_b, r_mu, r_w0, r_w2, r_a0, r_a2, r_g2, r_kk, r_ka, r_rk, r_ln_g, r_ln_b, m_cq_g, m_w_uq, m_ckv_g, m_w_ukv, m_qn_g, m_qr_g, m_kn_g, m_kr_g, w_router, router_bias, moe_w1, moe_w3, moe_w2):
    B, L, D = x.shape
    Lc = ctx.shape[1]
    depth = w_mod.shape[0]
    nlat, nctx = B * L, B * Lc
    T = nlat + nctx
    cos, sin = _rope_tables(L // GRID_W)
    tm_in = _tile(math.gcd(L, nctx), 512, 2 * SUBLANES)
    bd = jnp.kron(jnp.eye(LANES // HEAD_R, dtype=F32), jnp.ones((HEAD_R, HEAD_R), F32))

    H = (x.reshape(nlat, D), ctx.reshape(nctx, D))
    crow = jnp.concatenate([c, c_ctx[None, :], jnp.zeros((SUBLANES - (B + 1) % SUBLANES, D), F32)], 0)

    for l in range(depth):
        last = l == depth - 1
        mod = adaln_mod(crow, w_mod, b_mod, l)[:B + 1]
        sh1, sc1, ga1, sh2, sc2, ga2 = jnp.split(mod, 6, axis=-1)
        n = norm_mod(H, norm1_g[l], sc1, sh1, L)

        wi = w_in[l]
        cols = lambda lo, hi: wi[:, lo:hi]
        w_conv = cast_bf16(w_in, l, OFF_CONV, OFF_MQ - OFF_CONV)
        w_mq = cast_bf16(w_in, l, OFF_MQ, OFF_RR - OFF_MQ)
        w_rw = cast_bf16(jnp.concatenate(
            [cols(OFF_RR, OFF_RG), cols(OFF_RK, OFF_RV), cols(OFF_RV, OFF_RW), cols(OFF_RW, OFF_RA),
             cols(OFF_RA, OFF_MKV), cols(OFF_RG, OFF_RK), jnp.zeros((D, G_PAD - G_LORA), F32)], 1))
        w_kv = cast_bf16(jnp.concatenate(
            [cols(OFF_MKV, OFF_MKR), cols(OFF_MKR, IN_COLS), _rot_cols(cols(OFF_MKR, IN_COLS))], 1))
        u = matmul_plain(n, w_conv, name="in_conv")
        pr = matmul_plain(n, w_rw, name="in_rwkv")
        cq = matmul(n, w_mq, epi=_epi_mq, tm=tm_in, tn=Q_LORA,
                    extra=(m_cq_g[l].reshape(1, Q_LORA),),
                    extra_specs=(pl.BlockSpec((1, Q_LORA), lambda i, j, k: (0, 0)),),
                    out_shapes=[jax.ShapeDtypeStruct((T, Q_LORA), BF16)],
                    out_specs=[pl.BlockSpec((tm_in, Q_LORA), lambda i, j, k: (i, 0))], name="in_mq")[0]
        ktab = jnp.concatenate([_rope_tab(m_kr_g[l], cos, sin),
                                jnp.broadcast_to(_rope_tab(m_kr_g[l], None, None), (tm_in, 2 * ROPE))], 0)
        nlb = L // tm_in
        ckv, kr = matmul(n, w_kv, epi=_epi_mkv, tm=tm_in, tn=KV_LORA + 2 * ROPE,
                         extra=(m_ckv_g[l].reshape(1, KV_LORA), ktab),
                         extra_specs=(pl.BlockSpec((1, KV_LORA), lambda i, j, k: (0, 0)),
                                      pl.BlockSpec((tm_in, 2 * ROPE),
                                                   lambda i, j, k: (jnp.where(i < B * nlb, i % nlb, nlb), 0))),
                         out_shapes=[jax.ShapeDtypeStruct((T, KV_LORA), BF16), jax.ShapeDtypeStruct((T, ROPE), F32)],
                         out_specs=[pl.BlockSpec((tm_in, KV_LORA), lambda i, j, k: (i, 0)),
                                    pl.BlockSpec((tm_in, ROPE), lambda i, j, k: (i, 0))], name="in_mkv")

        conv = conformer_conv(u, conv_dw[l], conv_b[l], conv_ln_g[l], conv_ln_b[l], B, L, Lc, not last)

        mu = r_mu[l]
        mcols = lambda lo, hi: mu[:, lo - OFF_RR:hi - OFF_RR]
        mu_r = jnp.concatenate([mcols(OFF_RR, OFF_RG), mcols(OFF_RK, OFF_RV), mcols(OFF_RV, OFF_RW),
                                mcols(OFF_RW, OFF_RA), mcols(OFF_RA, OFF_MKV), mcols(OFF_RG, OFF_RK),
                                jnp.zeros((2, G_PAD - G_LORA), F32)], 1)
        rp = {"mu_prev": mu_r[0:1], "mu_next": mu_r[1:2],
              "w0": r_w0[l].reshape(1, 2 * C_R), "w2": _blockdiag2(r_w2[l]),
              "a0": r_a0[l].reshape(1, 2 * C_R), "a2": _blockdiag2(r_a2[l]),
              "kk": r_kk[l].reshape(1, C_R), "ka": r_ka[l].reshape(1, C_R), "rk": r_rk[l].reshape(1, C_R),
              "g2": jnp.pad(r_g2[l], ((0, G_PAD - G_LORA), (0, 0))), "bd": bd}
        p1f, p1b, p2f, p2b, p3, bonus, gate = rwkv_features(pr, rp, B, L, Lc)
        yf, yb = wkv_scan(p1f, p1b, p2f, p2b, p3, Lc)
        rw = rwkv_output(yf, yb, bonus, gate, r_ln_g[l], r_ln_b[l], B, L, Lc, not last)

        wq = m_w_uq[l].reshape(Q_LORA, N_HEADS_M, NOPE + ROPE)
        wq = cast_bf16(jnp.concatenate([wq, _rot_cols(wq[..., NOPE:])], -1).reshape(Q_LORA, N_HEADS_M * QH))
        wkv = cast_bf16(m_w_ukv, l)
        k_c, v_c = mla_kv_up(ckv, wkv, m_kn_g[l], kr, B, Lc, nlat)
        k_l, v_l = mla_kv_up(ckv, wkv, m_kn_g[l], kr, B, L, 0)
        q_l = mla_q_up(cq, wq, m_qn_g[l], _rope_tab(m_qr_g[l], cos, sin), B, L, 0)
        rows_out = nlat if last else T
        mla = mla_attention(q_l, [(k_c, v_c), (k_l, v_l)], rows_out, 0,
                            prev=None if last else jnp.zeros((T, C_M), BF16))
        if not last:
            tq_c = _tile(Lc, 512, 2 * SUBLANES)
            q_c = mla_q_up(cq, wq, m_qn_g[l], jnp.broadcast_to(_rope_tab(m_qr_g[l], None, None), (tq_c, 2 * ROPE)),
                           B, Lc, nlat)
            mla = mla_attention(q_c, [(k_c, v_c)], rows_out, nlat, prev=mla)

        Hn = out_proj((conv, rw, mla), cast_bf16(w_out, l), H, ga1, rows_out, L)
        n2, logits = norm_mod(Hn, norm2_g[l], sc2, sh2, L, w_router=w_router)
        H = moe_ffn(n2, logits[:, :N_EXPERTS], router_bias,
                    cast_bf16(moe_w1, l), cast_bf16(moe_w3, l), cast_bf16(moe_w2, l), Hn, ga2, L)
    return H[:nlat].reshape(B, L, D)
```
